```python
import math
import jax, jax.numpy as jnp
from jax import lax
import numpy as np


D_MODEL = 1024
BATCH = 8
SEQ = 4096
DEPTH = 1

CTX_LEN = 256
GRID_W = 64
RET_WIDTH = 512
LRU_WIDTH = 512
MIX_WIDTH = RET_WIDTH + LRU_WIDTH
RET_HEADS = 4
RET_HEAD_DIM = RET_WIDTH // RET_HEADS
RET_CHUNK = 128
LRU_BLOCKS = 8
LRU_BLOCK_DIM = LRU_WIDTH // LRU_BLOCKS
LRU_C = 8.0
CONV_WIDTH = 4
MLP_HIDDEN = 4 * D_MODEL
ROPE_BASE = 10000.0
NORM_EPS = 1e-6
N_MOD = 6
IN_COLS = 4 * RET_WIDTH + 2 * LRU_WIDTH
IN_SPLITS = (RET_WIDTH, 2 * RET_WIDTH, 3 * RET_WIDTH, 4 * RET_WIDTH, 4 * RET_WIDTH + LRU_WIDTH)

kernel_name = "hybrid_retention_rglru_dit_block"


def rmsnorm(x, g):
    xf = x.astype(jnp.float32)
    y = xf * lax.rsqrt(jnp.mean(xf * xf, axis=-1, keepdims=True) + NORM_EPS)
    return (y * g.astype(jnp.float32)).astype(x.dtype)


def modulate(h, shift, scale):
    return h * (1.0 + scale) + shift


def to_heads(a):
    b, t, _ = a.shape
    return a.reshape(b, t, RET_HEADS, RET_HEAD_DIM).astype(jnp.float32)


def head_groupnorm(y):
    mu = jnp.mean(y, axis=-1, keepdims=True)
    yc = y - mu
    var = jnp.mean(yc * yc, axis=-1, keepdims=True)
    return yc * lax.rsqrt(var + NORM_EPS)


def axial_rotary_tables(t_len):
    rows = t_len // GRID_W
    row = jnp.repeat(jnp.arange(rows, dtype=jnp.float32), GRID_W)
    col = jnp.tile(jnp.arange(GRID_W, dtype=jnp.float32), rows)
    n_freq = RET_HEAD_DIM // 4
    inv = ROPE_BASE ** (-jnp.arange(n_freq, dtype=jnp.float32) / n_freq)
    ang = jnp.concatenate([row[:, None] * inv, col[:, None] * inv], axis=-1)
    return jnp.cos(ang), jnp.sin(ang)


def apply_rotary(x, cos, sin):
    half = RET_HEAD_DIM // 2
    x1, x2 = x[..., :half], x[..., half:]
    c = cos[None, :, None, :]
    s = sin[None, :, None, :]
    return jnp.concatenate([x1 * c - x2 * s, x2 * c + x1 * s], axis=-1)


def retention_scan(q, k, v, log_g, s0):
    b, t, h, dh = q.shape
    n = t // RET_CHUNK

    def chunks(a):
        return a.reshape(b, n, RET_CHUNK, h, dh).transpose(0, 3, 1, 2, 4)

    qc, kc, vc = chunks(q), chunks(k), chunks(v)
    pos = jnp.arange(RET_CHUNK, dtype=jnp.float32)
    lg = log_g[:, None]
    rel = pos[:, None] - pos[None, :]
    decay = jnp.where(rel >= 0, jnp.exp(lg[:, :, None] * jnp.maximum(rel, 0.0)), 0.0)
    scores = jnp.einsum('bhncd,bhnmd->bhncm', qc, kc) * decay[None, :, None]
    intra = jnp.einsum('bhncm,bhnme->bhnce', scores, vc)
    w_state = jnp.exp(lg * (RET_CHUNK - 1.0 - pos))
    u = jnp.einsum('bhncd,bhnce->bhnde', kc * w_state[None, :, None, :, None], vc)
    g_chunk = jnp.exp(lg * RET_CHUNK)[None, :, :, None]

    def step(s, u_n):
        return g_chunk * s + u_n, s

    _, s_prev = lax.scan(step, s0, u.transpose(2, 0, 1, 3, 4))
    s_prev = s_prev.transpose(1, 2, 0, 3, 4)
    w_query = jnp.exp(lg * (pos + 1.0))
    cross = jnp.einsum('bhncd,bhnde->bhnce', qc * w_query[None, :, None, :, None], s_prev)
    return (intra + cross).transpose(0, 2, 3, 1, 4).reshape(b, t, h, dh)


def retention_bidir(q, k, v, log_g_f, log_g_b, s0_f, s0_b):
    o_f = retention_scan(q, k, v, log_g_f, s0_f)
    o_b = retention_scan(q[:, ::-1], k[:, ::-1], v[:, ::-1], log_g_b, s0_b)[:, ::-1]
    return o_f + o_b


def retention_final_state(k, v, log_g, reverse):
    l_len = k.shape[1]
    pos = jnp.arange(l_len, dtype=jnp.float32)
    steps = pos if reverse else (l_len - 1.0 - pos)
    w = jnp.exp(log_g[:, None] * steps[None, :])
    return jnp.einsum('blhd,blhe->bhde', k * w.T[None, :, :, None], v)


def centred_depthwise_conv(x, w, bias):
    t_len = x.shape[1]
    left = (CONV_WIDTH - 1) // 2
    right = CONV_WIDTH - 1 - left
    xp = jnp.pad(x, ((0, 0), (left, right), (0, 0)))
    out = bias + xp[:, 0:t_len] * w[0]
    for j in range(1, CONV_WIDTH):
        out = out + xp[:, j:j + t_len] * w[j]
    return out


def linear_scan(a, b, h0, reverse):
    if reverse:
        a, b = a[:, ::-1], b[:, ::-1]
    b = b.at[:, 0].add(a[:, 0] * h0)

    def combine(lhs, rhs):
        return lhs[0] * rhs[0], rhs[0] * lhs[1] + rhs[1]

    _, h = lax.associative_scan(combine, (a, b), axis=1)
    final = h[:, -1]
    if reverse:
        h = h[:, ::-1]
    return h, final


def rglru_direction(xc, w_a, b_a, w_x, b_x, lam, h0, reverse):
    b, t, _ = xc.shape
    xf = xc.astype(jnp.float32)
    xb = xf.reshape(b, t, LRU_BLOCKS, LRU_BLOCK_DIM)
    r = jax.nn.sigmoid(jnp.einsum('btnd,nde->btne', xb, w_a.astype(jnp.float32)).reshape(b, t, LRU_WIDTH) + b_a.astype(jnp.float32))
    i = jax.nn.sigmoid(jnp.einsum('btnd,nde->btne', xb, w_x.astype(jnp.float32)).reshape(b, t, LRU_WIDTH) + b_x.astype(jnp.float32))
    log_a = -LRU_C * r * jax.nn.softplus(-lam.astype(jnp.float32))
    a = jnp.exp(log_a)
    inp = jnp.sqrt(-jnp.expm1(2.0 * log_a)) * (i * xf)
    return linear_scan(a, inp, h0, reverse)


def mix_output(o_ret, g, h_lru, gate, w_out):
    b, t = g.shape[0], g.shape[1]
    ret = jax.nn.silu(g.astype(jnp.float32)) * head_groupnorm(o_ret).reshape(b, t, RET_WIDTH)
    lru = h_lru * jax.nn.gelu(gate.astype(jnp.float32))
    return jnp.concatenate([ret, lru], axis=-1).astype(w_out.dtype) @ w_out


def squared_relu_mlp(h, w1, w2):
    return jnp.square(jax.nn.relu(h @ w1)) @ w2


def setup_inputs(seed: int = 0) -> dict:
    key = jax.random.key(seed)
    ks = jax.random.split(key, 24)
    f32 = jnp.float32

    def nrm(k, shape, scale):
        return jax.random.normal(k, shape, f32) * scale

    x = nrm(ks[0], (BATCH, SEQ, D_MODEL), 1.0)
    c = nrm(ks[1], (BATCH, D_MODEL), 1.0)
    ctx = nrm(ks[2], (BATCH, CTX_LEN, D_MODEL), 1.0)
    c_ctx = nrm(ks[3], (D_MODEL,), 1.0)
    w_ada = nrm(ks[4], (DEPTH, D_MODEL, N_MOD * D_MODEL), 0.5 * D_MODEL ** -0.5)
    b_ada = nrm(ks[5], (DEPTH, N_MOD * D_MODEL), 0.01)
    norm1_g = 1.0 + nrm(ks[6], (DEPTH, D_MODEL), 0.02)
    norm2_g = 1.0 + nrm(ks[7], (DEPTH, D_MODEL), 0.02)
    w_in = nrm(ks[8], (DEPTH, D_MODEL, IN_COLS), D_MODEL ** -0.5)
    gamma = 1.0 - 2.0 ** (-5.0 - jnp.arange(RET_HEADS, dtype=f32))
    ret_decay = jnp.log(gamma) - jnp.log1p(-gamma) + nrm(ks[9], (DEPTH, 2, RET_HEADS), 0.05)
    conv_w = nrm(ks[10], (DEPTH, CONV_WIDTH, LRU_WIDTH), CONV_WIDTH ** -0.5)
    conv_b = nrm(ks[11], (DEPTH, LRU_WIDTH), 0.01)
    lru_wa = nrm(ks[12], (DEPTH, 2, LRU_BLOCKS, LRU_BLOCK_DIM, LRU_BLOCK_DIM), LRU_BLOCK_DIM ** -0.5)
    lru_ba = nrm(ks[13], (DEPTH, 2, LRU_WIDTH), 0.01)
    lru_wx = nrm(ks[14], (DEPTH, 2, LRU_BLOCKS, LRU_BLOCK_DIM, LRU_BLOCK_DIM), LRU_BLOCK_DIM ** -0.5)
    lru_bx = nrm(ks[15], (DEPTH, 2, LRU_WIDTH), 0.01)
    u = jax.random.uniform(ks[16], (DEPTH, 2, LRU_WIDTH), f32, 0.9, 0.999)
    a0 = u ** (1.0 / LRU_C)
    lru_lambda = jnp.log(a0) - jnp.log1p(-a0)
    w_out = nrm(ks[17], (DEPTH, MIX_WIDTH, D_MODEL), MIX_WIDTH ** -0.5)
    w_mlp1 = nrm(ks[18], (DEPTH, D_MODEL, MLP_HIDDEN), D_MODEL ** -0.5)
    w_mlp2 = nrm(ks[19], (DEPTH, MLP_HIDDEN, D_MODEL), MLP_HIDDEN ** -0.5)
    final_g = 1.0 + nrm(ks[20], (D_MODEL,), 0.02)
    return dict(x=x, c=c, ctx=ctx, c_ctx=c_ctx, w_ada=w_ada, b_ada=b_ada, norm1_g=norm1_g,
                norm2_g=norm2_g, w_in=w_in, ret_decay=ret_decay, conv_w=conv_w, conv_b=conv_b,
                lru_wa=lru_wa, lru_ba=lru_ba, lru_wx=lru_wx, lru_bx=lru_bx, lru_lambda=lru_lambda,
                w_out=w_out, w_mlp1=w_mlp1, w_mlp2=w_mlp2, final_g=final_g)


def reference(x, c, ctx, c_ctx, w_ada, b_ada, norm1_g, norm2_g, w_in, ret_decay, conv_w, conv_b,
              lru_wa, lru_ba, lru_wx, lru_bx, lru_lambda, w_out, w_mlp1, w_mlp2, final_g):
    b, t_len, _ = x.shape
    cos, sin = axial_rotary_tables(t_len)
    k_scale = RET_HEAD_DIM ** -0.5
    silu_c = jax.nn.silu(c)
    silu_cc = jax.nn.silu(c_ctx)
    for l in range(DEPTH):
        last = l == DEPTH - 1
        mod = silu_c @ w_ada[l] + b_ada[l]
        mod_c = silu_cc @ w_ada[l] + b_ada[l]
        sh1, sc1, g1, sh2, sc2, g2 = [m[:, None] for m in jnp.split(mod, N_MOD, axis=-1)]
        csh1, csc1, cg1, csh2, csc2, cg2 = jnp.split(mod_c, N_MOD, axis=-1)
        w_q, w_k, w_v, w_g, w_x, w_gate = jnp.split(w_in[l], IN_SPLITS, axis=1)
        lg_f = jax.nn.log_sigmoid(ret_decay[l, 0].astype(jnp.float32))
        lg_b = jax.nn.log_sigmoid(ret_decay[l, 1].astype(jnp.float32))
        lru_f = (lru_wa[l, 0], lru_ba[l, 0], lru_wx[l, 0], lru_bx[l, 0], lru_lambda[l, 0])
        lru_b = (lru_wa[l, 1], lru_ba[l, 1], lru_wx[l, 1], lru_bx[l, 1], lru_lambda[l, 1])

        hc = modulate(rmsnorm(ctx, norm1_g[l]), csh1, csc1)
        kc = to_heads(hc @ w_k) * k_scale
        vc = to_heads(hc @ w_v)
        s_f = retention_final_state(kc, vc, lg_f, False)
        s_b = retention_final_state(kc, vc, lg_b, True)
        xcc = centred_depthwise_conv(hc @ w_x, conv_w[l], conv_b[l])
        zero_h = jnp.zeros((b, LRU_WIDTH), jnp.float32)
        hcf, lru_sf = rglru_direction(xcc, *lru_f, zero_h, False)
        hcb, lru_sb = rglru_direction(xcc, *lru_b, zero_h, True)
        if not last:
            qc = to_heads(hc @ w_q)
            zero_s = jnp.zeros((b, RET_HEADS, RET_HEAD_DIM, RET_HEAD_DIM), jnp.float32)
            o_c = retention_bidir(qc, kc, vc, lg_f, lg_b, zero_s, zero_s)
            yc = mix_output(o_c, hc @ w_g, hcf + hcb, hc @ w_gate, w_out[l])
            ctx_next = ctx + cg1 * yc
            hc2 = modulate(rmsnorm(ctx_next, norm2_g[l]), csh2, csc2)
            ctx_next = ctx_next + cg2 * squared_relu_mlp(hc2, w_mlp1[l], w_mlp2[l])

        h = modulate(rmsnorm(x, norm1_g[l]), sh1, sc1)
        q, k, v, g, xr, gate = jnp.split(h @ w_in[l], IN_SPLITS, axis=-1)
        q = apply_rotary(to_heads(q), cos, sin)
        k = apply_rotary(to_heads(k), cos, sin) * k_scale
        o = retention_bidir(q, k, to_heads(v), lg_f, lg_b, s_f, s_b)
        xcl = centred_depthwise_conv(xr, conv_w[l], conv_b[l])
        hf, _ = rglru_direction(xcl, *lru_f, lru_sf, False)
        hb, _ = rglru_direction(xcl, *lru_b, lru_sb, True)
        y = mix_output(o, g, hf + hb, gate, w_out[l])
        x = x + g1 * y
        h2 = modulate(rmsnorm(x, norm2_g[l]), sh2, sc2)
        x = x + g2 * squared_relu_mlp(h2, w_mlp1[l], w_mlp2[l])
        if not last:
            ctx = ctx_next
    return rmsnorm(x, final_g)
```

```python
import functools
import math

import jax
import jax.numpy as jnp
from jax import lax
from jax.experimental import pallas as pl
from jax.experimental.pallas import tpu as pltpu

F32 = jnp.float32
BF16 = jnp.bfloat16

RET_HEADS = 4
HEAD_DIM = 128
RET_WIDTH = RET_HEADS * HEAD_DIM
LRU_WIDTH = 512
LRU_BLOCK_DIM = 64
LRU_C = 8.0
CONV_WIDTH = 4
N_MOD = 6
GRID_W = 64
ROPE_BASE = 10000.0
NORM_EPS = 1e-6

LANES = 128
SUBLANES = 8
BF16_SUBLANES = 16
VMEM_LIMIT_BYTES = 56 * 1024 * 1024

RET_CHUNK = 128
ROW_TILE = 512
LRU_CHUNK = 128
LRU_PITCH = LRU_CHUNK + SUBLANES
MLP_CHUNK = 1024
MOD_COLS = 1536
MOD_ROWS = 16
LRU_PAIR = 2 * LRU_BLOCK_DIM
N_PAIRS = LRU_WIDTH // LRU_PAIR


def _params(*sem):
    return pltpu.CompilerParams(dimension_semantics=sem, vmem_limit_bytes=VMEM_LIMIT_BYTES)


def _resident(shape):
    nd = len(shape)
    return pl.BlockSpec(shape, lambda *_: (0,) * nd, pipeline_mode=pl.Buffered(1))


def _sigmoid(x):
    return 1.0 / (1.0 + jnp.exp(-x))


def _softplus(x):
    return jnp.maximum(x, 0.0) + jnp.log(1.0 + jnp.exp(-jnp.abs(x)))


def _rms_modulate(x, g, shift, scale):
    ms = jnp.mean(x * x, axis=-1, keepdims=True)
    return (x * lax.rsqrt(ms + NORM_EPS) * g) * (1.0 + scale) + shift


def _dot(a, b):
    return jnp.dot(a, b, preferred_element_type=F32)


def _dot_tn(a, b):
    return lax.dot_general(a, b, (((0,), (0,)), ((), ())), preferred_element_type=F32)


def _dot_nt(a, b):
    return lax.dot_general(a, b, (((1,), (1,)), ((), ())), preferred_element_type=F32)


def _mod_kernel(c_ref, w_ref, b_ref, o_ref):
    c = c_ref[...]
    s = c * _sigmoid(c)
    o_ref[...] = jnp.dot(s, w_ref[...], preferred_element_type=F32,
                         precision=lax.Precision.HIGHEST) + b_ref[...]


def _modulation(c_rows, w_ada, b_ada):
    d, n = w_ada.shape
    return pl.pallas_call(
        _mod_kernel,
        name="modulation",
        grid=(n // MOD_COLS,),
        in_specs=[pl.BlockSpec((MOD_ROWS, d), lambda j: (0, 0)),
                  pl.BlockSpec((d, MOD_COLS), lambda j: (0, j)),
                  pl.BlockSpec((1, MOD_COLS), lambda j: (0, j))],
        out_specs=pl.BlockSpec((MOD_ROWS, MOD_COLS), lambda j: (0, j)),
        out_shape=jax.ShapeDtypeStruct((MOD_ROWS, n), F32),
        compiler_params=_params("arbitrary"),
    )(c_rows, w_ada, b_ada)


def _gate_pair(xc_p, wg_ref, bg_ref, sp_ref, p, d):
    col = 2 * d * LRU_PAIR
    w = wg_ref[p, :, col:col + 2 * LRU_PAIR]
    gz = _dot(xc_p.astype(BF16), w) + bg_ref[p, :, col:col + 2 * LRU_PAIR]
    r = _sigmoid(gz[:, :LRU_PAIR])
    i = _sigmoid(gz[:, LRU_PAIR:])
    sp = _softplus(-sp_ref[d:d + 1, p * LRU_PAIR:(p + 1) * LRU_PAIR])
    log_a = (-LRU_C) * r * sp
    a = jnp.exp(log_a)
    inp = jnp.sqrt(1.0 - a * a) * (i * xc_p)
    return log_a, a, inp


def _conv_taps(x, x_m1, x_p1, x_p2, cw_ref, cb_ref):
    return (cb_ref[...] + x_m1 * cw_ref[0:1, :] + x * cw_ref[1:2, :]
            + x_p1 * cw_ref[2:3, :] + x_p2 * cw_ref[3:4, :])


def _ctx_kernel(ctx_ref, mod_ref, g1_ref, wk_ref, wv_ref, wx_ref, dec_ref, cw_ref, cb_ref,
                wg_ref, bg_ref, lam_ref, sf_ref, sb_ref, lru_ref):
    x = ctx_ref[0]
    n_ctx = x.shape[0]
    h = _rms_modulate(x, g1_ref[...], mod_ref[0, 0:1, :], mod_ref[0, 1:2, :]).astype(BF16)
    k = _dot(h, wk_ref[...]) * (HEAD_DIM ** -0.5)
    v = _dot(h, wv_ref[...])
    xr = _dot(h, wx_ref[...])

    row = lax.broadcasted_iota(jnp.int32, (n_ctx, HEAD_DIM), 0).astype(F32)
    for hd in range(RET_HEADS):
        sl = slice(hd * HEAD_DIM, (hd + 1) * HEAD_DIM)
        lg_f = -_softplus(-dec_ref[0, hd])[0:1, :]
        lg_b = -_softplus(-dec_ref[1, hd])[0:1, :]
        kh, vh = k[:, sl], v[:, sl].astype(BF16)
        w_f = jnp.exp(lg_f * (n_ctx - 1.0 - row))
        w_b = jnp.exp(lg_b * row)
        sf_ref[0, hd] = _dot_tn((kh * w_f).astype(BF16), vh)
        sb_ref[0, hd] = _dot_tn((kh * w_b).astype(BF16), vh)

    rowc = lax.broadcasted_iota(jnp.int32, (n_ctx, LRU_WIDTH), 0)
    x_m1 = jnp.where(rowc >= 1, pltpu.roll(xr, 1, axis=0), 0.0)
    x_p1 = jnp.where(rowc < n_ctx - 1, pltpu.roll(xr, n_ctx - 1, axis=0), 0.0)
    x_p2 = jnp.where(rowc < n_ctx - 2, pltpu.roll(xr, n_ctx - 2, axis=0), 0.0)
    xc = _conv_taps(xr, x_m1, x_p1, x_p2, cw_ref, cb_ref)

    r_i = lax.broadcasted_iota(jnp.int32, (n_ctx, n_ctx), 0)
    c_i = lax.broadcasted_iota(jnp.int32, (n_ctx, n_ctx), 1)
    later = (c_i > r_i).astype(F32)
    earlier = (c_i < r_i).astype(F32)
    for p in range(N_PAIRS):
        sl = slice(p * LRU_PAIR, (p + 1) * LRU_PAIR)
        for d, tri in ((0, later), (1, earlier)):
            log_a, _, inp = _gate_pair(xc[:, sl], wg_ref, bg_ref, lam_ref, p, d)
            cum = jnp.dot(tri, log_a, preferred_element_type=F32, precision=lax.Precision.HIGHEST)
            lru_ref[0, d:d + 1, sl] = jnp.sum(jnp.exp(cum) * inp, axis=0, keepdims=True)


def _context(ctx, mod3, g1, w_in_bf, dec, cw, cb, wg, bg, lam):
    b, n_ctx, d = ctx.shape
    kcol, vcol, xcol = 1, 2, 4
    state = jax.ShapeDtypeStruct((b, RET_HEADS, HEAD_DIM, HEAD_DIM), F32)
    return pl.pallas_call(
        _ctx_kernel,
        name="context",
        grid=(b,),
        in_specs=[pl.BlockSpec((1, n_ctx, d), lambda i: (i, 0, 0)),
                  pl.BlockSpec((1, N_MOD, d), lambda i: (b, 0, 0)),
                  pl.BlockSpec((1, d), lambda i: (0, 0)),
                  pl.BlockSpec((d, RET_WIDTH), lambda i: (0, kcol)),
                  pl.BlockSpec((d, RET_WIDTH), lambda i: (0, vcol)),
                  pl.BlockSpec((d, LRU_WIDTH), lambda i: (0, xcol)),
                  pl.BlockSpec(dec.shape, lambda i: (0, 0, 0, 0)),
                  pl.BlockSpec(cw.shape, lambda i: (0, 0)),
                  pl.BlockSpec(cb.shape, lambda i: (0, 0)),
                  pl.BlockSpec(wg.shape, lambda i: (0, 0, 0)),
                  pl.BlockSpec(bg.shape, lambda i: (0, 0, 0)),
                  pl.BlockSpec(lam.shape, lambda i: (0, 0))],
        out_specs=[pl.BlockSpec((1, RET_HEADS, HEAD_DIM, HEAD_DIM), lambda i: (i, 0, 0, 0)),
                   pl.BlockSpec((1, RET_HEADS, HEAD_DIM, HEAD_DIM), lambda i: (i, 0, 0, 0)),
                   pl.BlockSpec((1, 2, LRU_WIDTH), lambda i: (i, 0, 0))],
        out_shape=[state, state, jax.ShapeDtypeStruct((b, 2, LRU_WIDTH), F32)],
        compiler_params=_params("arbitrary"),
    )(ctx, mod3, g1, w_in_bf, w_in_bf, w_in_bf, dec, cw, cb, wg, bg, lam)


def _inproj_kernel(x_ref, mod_ref, g1_ref, w_ref, cos_ref, sin_ref, z_ref):
    h = _rms_modulate(x_ref[0], g1_ref[...], mod_ref[0, 0:1, :], mod_ref[0, 1:2, :]).astype(BF16)
    cos = cos_ref[...]
    sin = sin_ref[...]
    n_blocks = w_ref.shape[1] // RET_WIDTH
    for n in range(n_blocks):
        cols = slice(n * RET_WIDTH, (n + 1) * RET_WIDTH)
        zc = _dot(h, w_ref[:, cols])
        if n < 2:
            scale = 1.0 if n == 0 else HEAD_DIM ** -0.5
            for hd in range(RET_HEADS):
                hs = slice(hd * HEAD_DIM, (hd + 1) * HEAD_DIM)
                blk = zc[:, hs]
                rot = blk * cos + pltpu.roll(blk, HEAD_DIM // 2, axis=1) * sin
                z_ref[0, :, n * RET_WIDTH + hd * HEAD_DIM:n * RET_WIDTH + (hd + 1) * HEAD_DIM] = (
                    (rot * scale).astype(BF16))
        else:
            z_ref[0, :, cols] = zc.astype(BF16)


def _in_proj(x, mod3, g1, w_in_bf, cos2, sin2):
    b, t, d = x.shape
    n_cols = w_in_bf.shape[1]
    return pl.pallas_call(
        _inproj_kernel,
        name="in_proj",
        grid=(b, t // ROW_TILE),
        in_specs=[pl.BlockSpec((1, ROW_TILE, d), lambda i, j: (i, j, 0)),
                  pl.BlockSpec((1, N_MOD, d), lambda i, j: (i, 0, 0)),
                  _resident((1, d)),
                  _resident((d, n_cols)),
                  pl.BlockSpec((ROW_TILE, HEAD_DIM), lambda i, j: (j, 0)),
                  pl.BlockSpec((ROW_TILE, HEAD_DIM), lambda i, j: (j, 0))],
        out_specs=pl.BlockSpec((1, ROW_TILE, n_cols), lambda i, j: (i, j, 0)),
        out_shape=jax.ShapeDtypeStruct((b, t, n_cols), BF16),
        compiler_params=_params("arbitrary", "arbitrary"),
    )(x, mod3, g1, w_in_bf, cos2, sin2)


def _ret_kernel(q_ref, k_ref, v_ref, g_ref, sf_ref, sb_ref, dec_ref, o_ref, u_ref, s_ref):
    c = RET_CHUNK
    n_chunks = q_ref.shape[1] // c
    lg_f = -_softplus(-dec_ref[0, 0])[0:1, :]
    lg_b = -_softplus(-dec_ref[1, 0])[0:1, :]
    row = lax.broadcasted_iota(jnp.int32, (c, HEAD_DIM), 0).astype(F32)
    col = lax.broadcasted_iota(jnp.int32, (c, c), 1).astype(F32)
    rel = row - col
    decay = (jnp.where(rel >= 0, jnp.exp(lg_f * jnp.maximum(rel, 0.0)), 0.0)
             + jnp.where(rel <= 0, jnp.exp(lg_b * jnp.maximum(-rel, 0.0)), 0.0))
    ws_f = jnp.exp(lg_f * (c - 1.0 - row))
    ws_b = jnp.exp(lg_b * row)
    wq_f = jnp.exp(lg_f * (row + 1.0))
    wq_b = jnp.exp(lg_b * (c - row))
    gc_f = jnp.exp(lg_f * float(c))
    gc_b = jnp.exp(lg_b * float(c))

    def chunk(n):
        return pl.ds(pl.multiple_of(n * c, c), c)

    def incr(n, carry):
        kn = k_ref[0, chunk(n), :]
        vn = v_ref[0, chunk(n), :].astype(F32)
        vv = jnp.concatenate([(vn * ws_f).astype(BF16), (vn * ws_b).astype(BF16)], axis=1)
        u_ref[n] = _dot_tn(kn, vv)
        return carry

    lax.fori_loop(0, n_chunks, incr, 0)

    def scan(n, carry):
        s_f, s_b = carry
        m = n_chunks - 1 - n
        s_ref[n, :, 0:HEAD_DIM] = s_f.astype(BF16)
        s_ref[m, :, HEAD_DIM:2 * HEAD_DIM] = s_b.astype(BF16)
        s_f = gc_f * s_f + u_ref[n, :, 0:HEAD_DIM]
        s_b = gc_b * s_b + u_ref[m, :, HEAD_DIM:2 * HEAD_DIM]
        return s_f, s_b

    lax.fori_loop(0, n_chunks, scan, (sf_ref[0, 0], sb_ref[0, 0]))

    def out(n, carry):
        qn = q_ref[0, chunk(n), :]
        kn = k_ref[0, chunk(n), :]
        vn = v_ref[0, chunk(n), :]
        scores = _dot_nt(qn, kn) * decay
        cross = _dot(qn, s_ref[n])
        o = (_dot(scores.astype(BF16), vn)
             + wq_f * cross[:, 0:HEAD_DIM] + wq_b * cross[:, HEAD_DIM:2 * HEAD_DIM])
        mu = jnp.mean(o, axis=-1, keepdims=True)
        oc = o - mu
        var = jnp.mean(oc * oc, axis=-1, keepdims=True)
        y = oc * lax.rsqrt(var + NORM_EPS)
        gn = g_ref[0, chunk(n), :].astype(F32)
        o_ref[0, chunk(n), :] = (gn * _sigmoid(gn) * y).astype(BF16)
        return carry

    lax.fori_loop(0, n_chunks, out, 0)


def _retention(z, s_f, s_b, dec):
    b, t, _ = z.shape
    n_chunks = t // RET_CHUNK
    qcol, kcol, vcol, gcol = (i * RET_HEADS for i in range(4))
    seq = lambda c0: pl.BlockSpec((1, t, HEAD_DIM), lambda i, h: (i, 0, c0 + h))
    state = pl.BlockSpec((1, 1, HEAD_DIM, HEAD_DIM), lambda i, h: (i, h, 0, 0))
    return pl.pallas_call(
        _ret_kernel,
        name="retention",
        grid=(b, RET_HEADS),
        in_specs=[seq(qcol), seq(kcol), seq(vcol), seq(gcol), state, state,
                  pl.BlockSpec((2, 1, SUBLANES, LANES), lambda i, h: (0, h, 0, 0))],
        out_specs=pl.BlockSpec((1, t, HEAD_DIM), lambda i, h: (i, 0, h)),
        out_shape=jax.ShapeDtypeStruct((b, t, RET_WIDTH), BF16),
        scratch_shapes=[pltpu.VMEM((n_chunks, HEAD_DIM, 2 * HEAD_DIM), F32),
                        pltpu.VMEM((n_chunks, HEAD_DIM, 2 * HEAD_DIM), BF16)],
        compiler_params=_params("arbitrary", "arbitrary"),
    )(z, z, z, z, s_f, s_b, dec)


def _lru_kernel(xf_ref, pf_ref, nf_ref, xb_ref, pb_ref, nb_ref, cw_ref, cb_ref, wg_ref, bg_ref,
                lam_ref, h0_ref, hf_ref, hb_ref, a_s, b_s, hc_s):
    i = pl.program_id(0)
    n_steps = pl.num_programs(0)
    n_batch = xf_ref.shape[0]
    tc, pitch = LRU_CHUNK, LRU_PITCH
    halo = pf_ref.shape[1]

    @pl.when(i == 0)
    def _():
        hc_s[...] = h0_ref[...]

    row = lax.broadcasted_iota(jnp.int32, (tc, LRU_WIDTH), 0)

    def fill(d, x_ref, prev_ref, next_ref, has_prev, has_next):
        def per_batch(bi, carry):
            x = x_ref[bi].astype(F32)
            prev = jnp.where(has_prev, prev_ref[bi].astype(F32)[halo - 1:halo, :], 0.0)
            nxt = jnp.where(has_next, next_ref[bi].astype(F32)[0:2, :], 0.0)
            x_m1 = jnp.where(row == 0, prev, pltpu.roll(x, 1, axis=0))
            x_p1 = jnp.where(row == tc - 1, nxt[0:1, :], pltpu.roll(x, tc - 1, axis=0))
            x_p2 = jnp.where(row == tc - 1, nxt[1:2, :],
                             jnp.where(row == tc - 2, nxt[0:1, :], pltpu.roll(x, tc - 2, axis=0)))
            xc = _conv_taps(x, x_m1, x_p1, x_p2, cw_ref, cb_ref)
            rows = pl.ds(pl.multiple_of(bi * pitch, SUBLANES), tc)
            for p in range(N_PAIRS):
                _, a, inp = _gate_pair(xc[:, p * LRU_PAIR:(p + 1) * LRU_PAIR],
                                       wg_ref, bg_ref, lam_ref, p, d)
                a_s[d, p, rows, :] = a
                b_s[d, p, rows, :] = inp
            return carry

        lax.fori_loop(0, n_batch, per_batch, 0)

    not_first = i > 0
    not_last = i < n_steps - 1
    fill(0, xf_ref, pf_ref, nf_ref, not_first, not_last)
    fill(1, xb_ref, pb_ref, nb_ref, not_last, not_first)

    def step(t, carry):
        hf, hb = carry
        tb = tc - 1 - t
        new_f, new_b = [], []
        for p in range(N_PAIRS):
            rf = pl.ds(t, n_batch, stride=pitch)
            rb = pl.ds(tb, n_batch, stride=pitch)
            h = a_s[0, p, rf, :] * hf[p] + b_s[0, p, rf, :]
            b_s[0, p, rf, :] = h
            new_f.append(h)
            h = a_s[1, p, rb, :] * hb[p] + b_s[1, p, rb, :]
            b_s[1, p, rb, :] = h
            new_b.append(h)
        return tuple(new_f), tuple(new_b)

    init = tuple(tuple(hc_s[d, :, p * LRU_PAIR:(p + 1) * LRU_PAIR] for p in range(N_PAIRS))
                 for d in range(2))
    hf, hb = lax.fori_loop(0, tc, step, init, unroll=8)
    for p in range(N_PAIRS):
        hc_s[0, :, p * LRU_PAIR:(p + 1) * LRU_PAIR] = hf[p]
        hc_s[1, :, p * LRU_PAIR:(p + 1) * LRU_PAIR] = hb[p]

    for bi in range(n_batch):
        for p in range(N_PAIRS):
            sl = slice(p * LRU_PAIR, (p + 1) * LRU_PAIR)
            hf_ref[bi, :, sl] = b_s[0, p, bi * pitch:bi * pitch + tc, :].astype(BF16)
            hb_ref[bi, :, sl] = b_s[1, p, bi * pitch:bi * pitch + tc, :].astype(BF16)


def _rglru(z, cw, cb, wg, bg, lam, h0):
    b, t, _ = z.shape
    n = t // LRU_CHUNK
    xcol = 4
    per_chunk = LRU_CHUNK // BF16_SUBLANES
    n_halo = t // BF16_SUBLANES
    main = lambda f: pl.BlockSpec((b, LRU_CHUNK, LRU_WIDTH), lambda i: (0, f(i), xcol))
    halo = lambda f: pl.BlockSpec((b, BF16_SUBLANES, LRU_WIDTH), lambda i: (0, f(i), xcol))
    fwd = lambda i: i
    bwd = lambda i: n - 1 - i
    prev_of = lambda f: (lambda i: jnp.maximum(f(i) * per_chunk - 1, 0))
    next_of = lambda f: (lambda i: jnp.minimum((f(i) + 1) * per_chunk, n_halo - 1))
    const = lambda a: pl.BlockSpec(a.shape, lambda i: (0,) * a.ndim)
    out = lambda f: pl.BlockSpec((b, LRU_CHUNK, LRU_WIDTH), lambda i: (0, f(i), 0))
    rows = b * LRU_PITCH
    return pl.pallas_call(
        _lru_kernel,
        name="rglru",
        grid=(n,),
        in_specs=[main(fwd), halo(prev_of(fwd)), halo(next_of(fwd)),
                  main(bwd), halo(prev_of(bwd)), halo(next_of(bwd)),
                  const(cw), const(cb), const(wg), const(bg), const(lam), const(h0)],
        out_specs=[out(fwd), out(bwd)],
        out_shape=[jax.ShapeDtypeStruct((b, t, LRU_WIDTH), BF16)] * 2,
        scratch_shapes=[pltpu.VMEM((2, N_PAIRS, rows, LANES), F32),
                        pltpu.VMEM((2, N_PAIRS, rows, LANES), F32),
                        pltpu.VMEM((2, b, LRU_WIDTH), F32)],
        compiler_params=_params("arbitrary"),
    )(z, z, z, z, z, z, cw, cb, wg, bg, lam, h0)


def _out_kernel(ret_ref, hf_ref, hb_ref, gate_ref, x_ref, mod_ref, g2_ref, gf_ref,
                wo_ref, w1_ref, w2_ref, o_ref):
    gate = gate_ref[0].astype(F32)
    gelu = 0.5 * gate * (1.0 + jnp.tanh(math.sqrt(2.0 / math.pi)
                                        * (gate + 0.044715 * (gate * gate * gate))))
    lru = ((hf_ref[0].astype(F32) + hb_ref[0].astype(F32)) * gelu).astype(BF16)
    y = _dot(ret_ref[0], wo_ref[0:RET_WIDTH, :]) + _dot(lru, wo_ref[RET_WIDTH:, :])
    x1 = x_ref[0] + mod_ref[0, 2:3, :] * y
    h2 = _rms_modulate(x1, g2_ref[...], mod_ref[0, 3:4, :], mod_ref[0, 4:5, :]).astype(BF16)
    acc = jnp.zeros_like(x1)
    for j in range(w1_ref.shape[1] // MLP_CHUNK):
        cols = slice(j * MLP_CHUNK, (j + 1) * MLP_CHUNK)
        hid = jnp.maximum(_dot(h2, w1_ref[:, cols]), 0.0)
        acc = acc + _dot((hid * hid).astype(BF16), w2_ref[cols, :])
    x2 = x1 + mod_ref[0, 5:6, :] * acc
    ms = jnp.mean(x2 * x2, axis=-1, keepdims=True)
    o_ref[0] = x2 * lax.rsqrt(ms + NORM_EPS) * gf_ref[...]


def _out_mlp(ret, hf, hb, z, x, mod3, g2, gf, wo, w1, w2):
    b, t, d = x.shape
    gatecol = 5
    tile = lambda w, c: pl.BlockSpec((1, ROW_TILE, w), lambda i, j: (i, j, c))
    return pl.pallas_call(
        _out_kernel,
        name="out_mlp",
        grid=(b, t // ROW_TILE),
        in_specs=[tile(RET_WIDTH, 0), tile(LRU_WIDTH, 0), tile(LRU_WIDTH, 0),
                  tile(LRU_WIDTH, gatecol), tile(d, 0),
                  pl.BlockSpec((1, N_MOD, d), lambda i, j: (i, 0, 0)),
                  _resident((1, d)), _resident((1, d)),
                  _resident(wo.shape), _resident(w1.shape), _resident(w2.shape)],
        out_specs=tile(d, 0),
        out_shape=jax.ShapeDtypeStruct((b, t, d), F32),
        compiler_params=_params("arbitrary", "arbitrary"),
    )(ret, hf, hb, z, x, mod3, g2, gf, wo, w1, w2)


def _rotary_tables(t_len):
    rows = t_len // GRID_W
    row = jnp.repeat(jnp.arange(rows, dtype=F32), GRID_W)
    col = jnp.tile(jnp.arange(GRID_W, dtype=F32), rows)
    n_freq = HEAD_DIM // 4
    inv = ROPE_BASE ** (-jnp.arange(n_freq, dtype=F32) / n_freq)
    ang = jnp.concatenate([row[:, None] * inv, col[:, None] * inv], axis=-1)
    cos, sin = jnp.cos(ang), jnp.sin(ang)
    return jnp.concatenate([cos, cos], axis=-1), jnp.concatenate([-sin, sin], axis=-1)


def _pair_gate_weights(wa, wx, ba, bx):
    def pair_blocks(w):
        w = w.reshape(N_PAIRS, 2, LRU_BLOCK_DIM, LRU_BLOCK_DIM)
        zero = jnp.zeros_like(w[:, 0])
        top = jnp.concatenate([w[:, 0], zero], axis=-1)
        bot = jnp.concatenate([zero, w[:, 1]], axis=-1)
        return jnp.concatenate([top, bot], axis=-2)

    wg = jnp.concatenate([pair_blocks(wa[0]), pair_blocks(wx[0]),
                          pair_blocks(wa[1]), pair_blocks(wx[1])], axis=-1)
    bias = lambda v: v.reshape(N_PAIRS, 1, LRU_PAIR)
    bg = jnp.concatenate([bias(ba[0]), bias(bx[0]), bias(ba[1]), bias(bx[1])], axis=-1)
    return wg.astype(BF16), bg


def kernel(x, c, ctx, c_ctx, w_ada, b_ada, norm1_g, norm2_g, w_in, ret_decay, conv_w, conv_b,
           lru_wa, lru_ba, lru_wx, lru_bx, lru_lambda, w_out, w_mlp1, w_mlp2, final_g):
    b, t_len, d = x.shape
    depth = w_ada.shape[0]
    assert depth == 1 and b + 1 <= MOD_ROWS
    assert t_len % ROW_TILE == 0 and t_len % LRU_CHUNK == 0 and t_len % RET_CHUNK == 0
    l = 0
    cos2, sin2 = _rotary_tables(t_len)

    c_rows = jnp.zeros((MOD_ROWS, d), F32).at[:b].set(c).at[b].set(c_ctx)
    mod = _modulation(c_rows, w_ada[l], b_ada[l][None, :])
    mod3 = mod.reshape(MOD_ROWS, N_MOD, d)

    w_in_bf = w_in[l].astype(BF16)
    g1 = norm1_g[l][None, :]
    dec = jnp.broadcast_to(ret_decay[l][:, :, None, None], (2, RET_HEADS, SUBLANES, LANES))
    cw, cb = conv_w[l], conv_b[l][None, :]
    wg, bg = _pair_gate_weights(lru_wa[l], lru_wx[l], lru_ba[l], lru_bx[l])
    lam = lru_lambda[l]

    s_f, s_b, lru_s = _context(ctx, mod3, g1, w_in_bf, dec, cw, cb, wg, bg, lam)
    z = _in_proj(x, mod3, g1, w_in_bf, cos2, sin2)
    ret = _retention(z, s_f, s_b, dec)
    hf, hb = _rglru(z, cw, cb, wg, bg, lam, jnp.transpose(lru_s, (1, 0, 2)))
    return _out_mlp(ret, hf, hb, z, x, mod3, norm2_g[l][None, :], final_g[None, :],
                    w_out[l].astype(BF16), w_mlp1[l].astype(BF16), w_mlp2[l].astype(BF16))
```

```python
import functools
import math

import jax
import jax.numpy as jnp
from jax import lax
from jax.experimental import pallas as pl
from jax.experimental.pallas import tpu as pltpu

F32 = jnp.float32
BF16 = jnp.bfloat16

RET_HEADS = 4
HEAD_DIM = 128
RET_WIDTH = RET_HEADS * HEAD_DIM
LRU_WIDTH = 512
LRU_BLOCK_DIM = 64
LRU_C = 8.0
CONV_WIDTH = 4
N_MOD = 6
GRID_W = 64
ROPE_BASE = 10000.0
NORM_EPS = 1e-6

LANES = 128
SUBLANES = 8
BF16_SUBLANES = 16
VMEM_LIMIT_BYTES = 56 * 1024 * 1024

RET_CHUNK = 256
RET_UNROLL = 4
ROW_TILE = 512
LRU_CHUNK = 128
LRU_PITCH = LRU_CHUNK + SUBLANES
MLP_CHUNK = 1024
MOD_COLS = 1536
MOD_ROWS = 16
LRU_PAIR = 2 * LRU_BLOCK_DIM
N_PAIRS = LRU_WIDTH // LRU_PAIR


def _params(*sem):
    return pltpu.CompilerParams(dimension_semantics=sem, vmem_limit_bytes=VMEM_LIMIT_BYTES)


def _resident(shape):
    nd = len(shape)
    return pl.BlockSpec(shape, lambda *_: (0,) * nd, pipeline_mode=pl.Buffered(1))


def _sigmoid(x):
    return 0.5 * jnp.tanh(0.5 * x) + 0.5


def _softplus(x):
    return jnp.maximum(x, 0.0) + jnp.log(1.0 + jnp.exp(-jnp.abs(x)))


def _rms_modulate(x, g, shift, scale):
    ms = jnp.mean(x * x, axis=-1, keepdims=True)
    return (x * lax.rsqrt(ms + NORM_EPS) * g) * (1.0 + scale) + shift


def _dot(a, b):
    return jnp.dot(a, b, preferred_element_type=F32)


def _dot_tn(a, b):
    return lax.dot_general(a, b, (((0,), (0,)), ((), ())), preferred_element_type=F32)


def _dot_nt(a, b):
    return lax.dot_general(a, b, (((1,), (1,)), ((), ())), preferred_element_type=F32)


def _mod_kernel(c_ref, w_ref, b_ref, o_ref):
    c = c_ref[...]
    s = c * _sigmoid(c)
    o_ref[...] = jnp.dot(s, w_ref[...], preferred_element_type=F32,
                         precision=lax.Precision.HIGHEST) + b_ref[...]


def _modulation(c_rows, w_ada, b_ada):
    d, n = w_ada.shape
    return pl.pallas_call(
        _mod_kernel,
        name="modulation",
        grid=(n // MOD_COLS,),
        in_specs=[pl.BlockSpec((MOD_ROWS, d), lambda j: (0, 0)),
                  pl.BlockSpec((d, MOD_COLS), lambda j: (0, j)),
                  pl.BlockSpec((1, MOD_COLS), lambda j: (0, j))],
        out_specs=pl.BlockSpec((MOD_ROWS, MOD_COLS), lambda j: (0, j)),
        out_shape=jax.ShapeDtypeStruct((MOD_ROWS, n), F32),
        compiler_params=_params("arbitrary"),
    )(c_rows, w_ada, b_ada)


def _gate_pair(xc_p, wg_ref, bg_ref, sp_ref, p, d):
    col = 2 * d * LRU_PAIR
    w = wg_ref[p, :, col:col + 2 * LRU_PAIR]
    gz = _dot(xc_p.astype(BF16), w) + bg_ref[p, :, col:col + 2 * LRU_PAIR]
    r = _sigmoid(gz[:, :LRU_PAIR])
    i = _sigmoid(gz[:, LRU_PAIR:])
    neg_c_sp = (-LRU_C) * _softplus(-sp_ref[d:d + 1, p * LRU_PAIR:(p + 1) * LRU_PAIR])
    log_a = r * neg_c_sp
    a = jnp.exp(log_a)
    s = 1.0 - a * a
    inp = jnp.where(s > 0.0, s * lax.rsqrt(s), 0.0) * (i * xc_p)
    return log_a, a, inp


def _conv_taps(x, x_m1, x_p1, x_p2, cw_ref, cb_ref):
    return (cb_ref[...] + x_m1 * cw_ref[0:1, :] + x * cw_ref[1:2, :]
            + x_p1 * cw_ref[2:3, :] + x_p2 * cw_ref[3:4, :])


def _ctx_kernel(ctx_ref, mod_ref, g1_ref, wk_ref, wv_ref, wx_ref, dec_ref, cw_ref, cb_ref,
                wg_ref, bg_ref, lam_ref, sf_ref, sb_ref, lru_ref):
    x = ctx_ref[0]
    n_ctx = x.shape[0]
    h = _rms_modulate(x, g1_ref[...], mod_ref[0, 0:1, :], mod_ref[0, 1:2, :]).astype(BF16)
    k = _dot(h, wk_ref[...]) * (HEAD_DIM ** -0.5)
    v = _dot(h, wv_ref[...])
    xr = _dot(h, wx_ref[...])

    row = lax.broadcasted_iota(jnp.int32, (n_ctx, HEAD_DIM), 0).astype(F32)
    for hd in range(RET_HEADS):
        sl = slice(hd * HEAD_DIM, (hd + 1) * HEAD_DIM)
        lg_f = -_softplus(-dec_ref[0, hd])[0:1, :]
        lg_b = -_softplus(-dec_ref[1, hd])[0:1, :]
        kh, vh = k[:, sl], v[:, sl].astype(BF16)
        w_f = jnp.exp(lg_f * (n_ctx - 1.0 - row))
        w_b = jnp.exp(lg_b * row)
        sf_ref[0, hd] = _dot_tn((kh * w_f).astype(BF16), vh)
        sb_ref[0, hd] = _dot_tn((kh * w_b).astype(BF16), vh)

    rowc = lax.broadcasted_iota(jnp.int32, (n_ctx, LRU_WIDTH), 0)
    x_m1 = jnp.where(rowc >= 1, pltpu.roll(xr, 1, axis=0), 0.0)
    x_p1 = jnp.where(rowc < n_ctx - 1, pltpu.roll(xr, n_ctx - 1, axis=0), 0.0)
    x_p2 = jnp.where(rowc < n_ctx - 2, pltpu.roll(xr, n_ctx - 2, axis=0), 0.0)
    xc = _conv_taps(xr, x_m1, x_p1, x_p2, cw_ref, cb_ref)

    r_i = lax.broadcasted_iota(jnp.int32, (n_ctx, n_ctx), 0)
    c_i = lax.broadcasted_iota(jnp.int32, (n_ctx, n_ctx), 1)
    later = (c_i > r_i).astype(F32)
    earlier = (c_i < r_i).astype(F32)
    for p in range(N_PAIRS):
        sl = slice(p * LRU_PAIR, (p + 1) * LRU_PAIR)
        for d, tri in ((0, later), (1, earlier)):
            log_a, _, inp = _gate_pair(xc[:, sl], wg_ref, bg_ref, lam_ref, p, d)
            cum = jnp.dot(tri, log_a, preferred_element_type=F32, precision=lax.Precision.HIGHEST)
            lru_ref[0, d:d + 1, sl] = jnp.sum(jnp.exp(cum) * inp, axis=0, keepdims=True)


def _context(ctx, mod3, g1, w_in_bf, dec, cw, cb, wg, bg, lam):
    b, n_ctx, d = ctx.shape
    kcol, vcol, xcol = 1, 2, 4
    state = jax.ShapeDtypeStruct((b, RET_HEADS, HEAD_DIM, HEAD_DIM), F32)
    return pl.pallas_call(
        _ctx_kernel,
        name="context",
        grid=(b,),
        in_specs=[pl.BlockSpec((1, n_ctx, d), lambda i: (i, 0, 0)),
                  pl.BlockSpec((1, N_MOD, d), lambda i: (b, 0, 0)),
                  pl.BlockSpec((1, d), lambda i: (0, 0)),
                  pl.BlockSpec((d, RET_WIDTH), lambda i: (0, kcol)),
                  pl.BlockSpec((d, RET_WIDTH), lambda i: (0, vcol)),
                  pl.BlockSpec((d, LRU_WIDTH), lambda i: (0, xcol)),
                  pl.BlockSpec(dec.shape, lambda i: (0, 0, 0, 0)),
                  pl.BlockSpec(cw.shape, lambda i: (0, 0)),
                  pl.BlockSpec(cb.shape, lambda i: (0, 0)),
                  pl.BlockSpec(wg.shape, lambda i: (0, 0, 0)),
                  pl.BlockSpec(bg.shape, lambda i: (0, 0, 0)),
                  pl.BlockSpec(lam.shape, lambda i: (0, 0))],
        out_specs=[pl.BlockSpec((1, RET_HEADS, HEAD_DIM, HEAD_DIM), lambda i: (i, 0, 0, 0)),
                   pl.BlockSpec((1, RET_HEADS, HEAD_DIM, HEAD_DIM), lambda i: (i, 0, 0, 0)),
                   pl.BlockSpec((1, 2, LRU_WIDTH), lambda i: (i, 0, 0))],
        out_shape=[state, state, jax.ShapeDtypeStruct((b, 2, LRU_WIDTH), F32)],
        compiler_params=_params("arbitrary"),
    )(ctx, mod3, g1, w_in_bf, w_in_bf, w_in_bf, dec, cw, cb, wg, bg, lam)


def _inproj_kernel(x_ref, mod_ref, g1_ref, w_ref, cos_ref, sin_ref, z_ref):
    h = _rms_modulate(x_ref[0], g1_ref[...], mod_ref[0, 0:1, :], mod_ref[0, 1:2, :]).astype(BF16)
    cos = cos_ref[...]
    sin = sin_ref[...]
    n_blocks = w_ref.shape[1] // RET_WIDTH
    for n in range(n_blocks):
        cols = slice(n * RET_WIDTH, (n + 1) * RET_WIDTH)
        zc = _dot(h, w_ref[:, cols])
        if n < 2:
            scale = 1.0 if n == 0 else HEAD_DIM ** -0.5
            for hd in range(RET_HEADS):
                hs = slice(hd * HEAD_DIM, (hd + 1) * HEAD_DIM)
                blk = zc[:, hs]
                rot = blk * cos + pltpu.roll(blk, HEAD_DIM // 2, axis=1) * sin
                z_ref[0, :, n * RET_WIDTH + hd * HEAD_DIM:n * RET_WIDTH + (hd + 1) * HEAD_DIM] = (
                    (rot * scale).astype(BF16))
        else:
            z_ref[0, :, cols] = zc.astype(BF16)


def _in_proj(x, mod3, g1, w_in_bf, cos2, sin2):
    b, t, d = x.shape
    n_cols = w_in_bf.shape[1]
    return pl.pallas_call(
        _inproj_kernel,
        name="in_proj",
        grid=(b, t // ROW_TILE),
        in_specs=[pl.BlockSpec((1, ROW_TILE, d), lambda i, j: (i, j, 0)),
                  pl.BlockSpec((1, N_MOD, d), lambda i, j: (i, 0, 0)),
                  _resident((1, d)),
                  _resident((d, n_cols)),
                  pl.BlockSpec((ROW_TILE, HEAD_DIM), lambda i, j: (j, 0)),
                  pl.BlockSpec((ROW_TILE, HEAD_DIM), lambda i, j: (j, 0))],
        out_specs=pl.BlockSpec((1, ROW_TILE, n_cols), lambda i, j: (i, j, 0)),
        out_shape=jax.ShapeDtypeStruct((b, t, n_cols), BF16),
        compiler_params=_params("arbitrary", "arbitrary"),
    )(x, mod3, g1, w_in_bf, cos2, sin2)


def _ret_kernel(q_ref, k_ref, v_ref, sf_ref, sb_ref, dec_ref, o_ref, u_ref, s_ref, dec_s):
    c = RET_CHUNK
    n_chunks = q_ref.shape[1] // c
    lg_f = -_softplus(-dec_ref[0, 0])[0:1, :]
    lg_b = -_softplus(-dec_ref[1, 0])[0:1, :]
    lg_fc = jnp.concatenate([lg_f] * (c // LANES), axis=1)
    lg_bc = jnp.concatenate([lg_b] * (c // LANES), axis=1)
    row = lax.broadcasted_iota(jnp.int32, (c, HEAD_DIM), 0).astype(F32)
    rel = (lax.broadcasted_iota(jnp.int32, (c, c), 0)
           - lax.broadcasted_iota(jnp.int32, (c, c), 1)).astype(F32)
    dec_s[...] = (jnp.where(rel >= 0, jnp.exp(lg_fc * jnp.maximum(rel, 0.0)), 0.0)
                  + jnp.where(rel <= 0, jnp.exp(lg_bc * jnp.maximum(-rel, 0.0)), 0.0))
    ws_f = jnp.exp(lg_f * (c - 1.0 - row)).astype(BF16)
    ws_b = jnp.exp(lg_b * row).astype(BF16)
    wq_f = jnp.exp(lg_f * (row + 1.0)).astype(BF16)
    wq_b = jnp.exp(lg_b * (c - row)).astype(BF16)
    gc_f = jnp.exp(lg_f * float(c))
    gc_b = jnp.exp(lg_b * float(c))

    def chunk(n):
        return pl.ds(pl.multiple_of(n * c, c), c)

    def incr(n, carry):
        kn = k_ref[0, chunk(n), :]
        vn = v_ref[0, chunk(n), :]
        u_ref[n] = _dot_tn(kn, jnp.concatenate([vn * ws_f, vn * ws_b], axis=1))
        return carry

    lax.fori_loop(0, n_chunks, incr, 0, unroll=RET_UNROLL)

    def scan(n, carry):
        s_f, s_b = carry
        m = n_chunks - 1 - n
        s_ref[n, 0:HEAD_DIM, :] = s_f.astype(BF16)
        s_ref[m, HEAD_DIM:2 * HEAD_DIM, :] = s_b.astype(BF16)
        s_f = gc_f * s_f + u_ref[n, :, 0:HEAD_DIM]
        s_b = gc_b * s_b + u_ref[m, :, HEAD_DIM:2 * HEAD_DIM]
        return s_f, s_b

    lax.fori_loop(0, n_chunks, scan, (sf_ref[0, 0], sb_ref[0, 0]))

    def out(n, carry):
        qn = q_ref[0, chunk(n), :]
        kn = k_ref[0, chunk(n), :]
        vn = v_ref[0, chunk(n), :]
        scores = (_dot_nt(qn, kn) * dec_s[...]).astype(BF16)
        qw = jnp.concatenate([qn * wq_f, qn * wq_b], axis=1)
        o_ref[0, chunk(n), :] = (_dot(scores, vn) + _dot(qw, s_ref[n])).astype(BF16)
        return carry

    lax.fori_loop(0, n_chunks, out, 0, unroll=RET_UNROLL)


def _retention(z, s_f, s_b, dec):
    b, t, _ = z.shape
    n_chunks = t // RET_CHUNK
    qcol, kcol, vcol = (i * RET_HEADS for i in range(3))
    seq = lambda c0: pl.BlockSpec((1, t, HEAD_DIM), lambda i, h: (i, 0, c0 + h))
    state = pl.BlockSpec((1, 1, HEAD_DIM, HEAD_DIM), lambda i, h: (i, h, 0, 0))
    return pl.pallas_call(
        _ret_kernel,
        name="retention",
        grid=(b, RET_HEADS),
        in_specs=[seq(qcol), seq(kcol), seq(vcol), state, state,
                  pl.BlockSpec((2, 1, SUBLANES, LANES), lambda i, h: (0, h, 0, 0))],
        out_specs=pl.BlockSpec((1, t, HEAD_DIM), lambda i, h: (i, 0, h)),
        out_shape=jax.ShapeDtypeStruct((b, t, RET_WIDTH), BF16),
        scratch_shapes=[pltpu.VMEM((n_chunks, HEAD_DIM, 2 * HEAD_DIM), F32),
                        pltpu.VMEM((n_chunks, 2 * HEAD_DIM, HEAD_DIM), BF16),
                        pltpu.VMEM((RET_CHUNK, RET_CHUNK), F32)],
        compiler_params=_params("arbitrary", "arbitrary"),
    )(z, z, z, s_f, s_b, dec)


def _lru_kernel(xf_ref, pf_ref, nf_ref, xb_ref, pb_ref, nb_ref, cw_ref, cb_ref, wg_ref, bg_ref,
                lam_ref, h0_ref, hf_ref, hb_ref, a_s, b_s, h_s, hc_s, x_s):
    i = pl.program_id(0)
    n_steps = pl.num_programs(0)
    n_batch = xf_ref.shape[0]
    tc, pitch = LRU_CHUNK, LRU_PITCH
    halo = pf_ref.shape[1]

    @pl.when(i == 0)
    def _():
        hc_s[...] = h0_ref[...]

    def fill(d, x_ref, prev_ref, next_ref, has_prev, has_next):
        def per_batch(bi, carry):
            x = x_ref[bi].astype(F32)
            prev = jnp.where(has_prev, prev_ref[bi].astype(F32)[halo - 1:halo, :], 0.0)
            nxt = jnp.where(has_next, next_ref[bi].astype(F32)[0:2, :], 0.0)
            lo = SUBLANES
            for p in range(N_PAIRS):
                sl = slice(p * LRU_PAIR, (p + 1) * LRU_PAIR)
                x_s[p, lo:lo + tc, :] = x[:, sl]
                x_s[p, lo - 1:lo, :] = prev[:, sl]
                x_s[p, lo + tc:lo + tc + 2, :] = nxt[:, sl]
            rows = pl.ds(pl.multiple_of(bi * pitch, SUBLANES), tc)
            for p in range(N_PAIRS):
                sl = slice(p * LRU_PAIR, (p + 1) * LRU_PAIR)
                xc = cb_ref[:, sl]
                for j in range(CONV_WIDTH):
                    xc = xc + x_s[p, lo - 1 + j:lo - 1 + j + tc, :] * cw_ref[j:j + 1, sl]
                _, a, inp = _gate_pair(xc, wg_ref, bg_ref, lam_ref, p, d)
                a_s[d, p, rows, :] = a
                b_s[d, p, rows, :] = inp
            return carry

        lax.fori_loop(0, n_batch, per_batch, 0)

    not_first = i > 0
    not_last = i < n_steps - 1
    fill(0, xf_ref, pf_ref, nf_ref, not_first, not_last)
    fill(1, xb_ref, pb_ref, nb_ref, not_last, not_first)

    def step(t, carry):
        hf, hb = carry
        tb = tc - 1 - t
        new_f, new_b = [], []
        for p in range(N_PAIRS):
            rf = pl.ds(t, n_batch, stride=pitch)
            rb = pl.ds(tb, n_batch, stride=pitch)
            h = a_s[0, p, rf, :] * hf[p] + b_s[0, p, rf, :]
            h_s[0, p, rf, :] = h
            new_f.append(h)
            h = a_s[1, p, rb, :] * hb[p] + b_s[1, p, rb, :]
            h_s[1, p, rb, :] = h
            new_b.append(h)
        return tuple(new_f), tuple(new_b)

    init = tuple(tuple(hc_s[d, :, p * LRU_PAIR:(p + 1) * LRU_PAIR] for p in range(N_PAIRS))
                 for d in range(2))
    hf, hb = lax.fori_loop(0, tc, step, init, unroll=8)
    for p in range(N_PAIRS):
        hc_s[0, :, p * LRU_PAIR:(p + 1) * LRU_PAIR] = hf[p]
        hc_s[1, :, p * LRU_PAIR:(p + 1) * LRU_PAIR] = hb[p]

    for bi in range(n_batch):
        for p in range(N_PAIRS):
            sl = slice(p * LRU_PAIR, (p + 1) * LRU_PAIR)
            hf_ref[bi, :, sl] = h_s[0, p, bi * pitch:bi * pitch + tc, :].astype(BF16)
            hb_ref[bi, :, sl] = h_s[1, p, bi * pitch:bi * pitch + tc, :].astype(BF16)


def _rglru(z, cw, cb, wg, bg, lam, h0):
    b, t, _ = z.shape
    n = t // LRU_CHUNK
    xcol = 4
    per_chunk = LRU_CHUNK // BF16_SUBLANES
    n_halo = t // BF16_SUBLANES
    main = lambda f: pl.BlockSpec((b, LRU_CHUNK, LRU_WIDTH), lambda i: (0, f(i), xcol))
    halo = lambda f: pl.BlockSpec((b, BF16_SUBLANES, LRU_WIDTH), lambda i: (0, f(i), xcol))
    fwd = lambda i: i
    bwd = lambda i: n - 1 - i
    prev_of = lambda f: (lambda i: jnp.maximum(f(i) * per_chunk - 1, 0))
    next_of = lambda f: (lambda i: jnp.minimum((f(i) + 1) * per_chunk, n_halo - 1))
    const = lambda a: pl.BlockSpec(a.shape, lambda i: (0,) * a.ndim)
    out = lambda f: pl.BlockSpec((b, LRU_CHUNK, LRU_WIDTH), lambda i: (0, f(i), 0))
    rows = b * LRU_PITCH
    return pl.pallas_call(
        _lru_kernel,
        name="rglru",
        grid=(n,),
        in_specs=[main(fwd), halo(prev_of(fwd)), halo(next_of(fwd)),
                  main(bwd), halo(prev_of(bwd)), halo(next_of(bwd)),
                  const(cw), const(cb), const(wg), const(bg), const(lam), const(h0)],
        out_specs=[out(fwd), out(bwd)],
        out_shape=[jax.ShapeDtypeStruct((b, t, LRU_WIDTH), BF16)] * 2,
        scratch_shapes=[pltpu.VMEM((2, N_PAIRS, rows, LANES), F32),
                        pltpu.VMEM((2, N_PAIRS, rows, LANES), F32),
                        pltpu.VMEM((2, N_PAIRS, rows, LANES), F32),
                        pltpu.VMEM((2, b, LRU_WIDTH), F32),
                        pltpu.VMEM((N_PAIRS, LRU_CHUNK + 2 * SUBLANES, LANES), F32)],
        compiler_params=_params("arbitrary"),
    )(z, z, z, z, z, z, cw, cb, wg, bg, lam, h0)


def _out_kernel(ret_ref, g_ref, hf_ref, hb_ref, gate_ref, x_ref, mod_ref, g2_ref, gf_ref,
                wo_ref, w1_ref, w2_ref, o_ref):
    heads = []
    for hd in range(RET_HEADS):
        hs = slice(hd * HEAD_DIM, (hd + 1) * HEAD_DIM)
        o = ret_ref[0, :, hs].astype(F32)
        oc = o - jnp.mean(o, axis=-1, keepdims=True)
        var = jnp.mean(oc * oc, axis=-1, keepdims=True)
        g = g_ref[0, :, hs].astype(F32)
        heads.append((g * _sigmoid(g) * (oc * lax.rsqrt(var + NORM_EPS))).astype(BF16))
    ret = jnp.concatenate(heads, axis=1)
    gate = gate_ref[0].astype(F32)
    gelu = 0.5 * gate * (1.0 + jnp.tanh(math.sqrt(2.0 / math.pi)
                                        * (gate + 0.044715 * (gate * gate * gate))))
    lru = ((hf_ref[0].astype(F32) + hb_ref[0].astype(F32)) * gelu).astype(BF16)
    y = _dot(ret, wo_ref[0:RET_WIDTH, :]) + _dot(lru, wo_ref[RET_WIDTH:, :])
    x1 = x_ref[0] + mod_ref[0, 2:3, :] * y
    h2 = _rms_modulate(x1, g2_ref[...], mod_ref[0, 3:4, :], mod_ref[0, 4:5, :]).astype(BF16)
    acc = jnp.zeros_like(x1)
    for j in range(w1_ref.shape[1] // MLP_CHUNK):
        cols = slice(j * MLP_CHUNK, (j + 1) * MLP_CHUNK)
        hid = jnp.maximum(_dot(h2, w1_ref[:, cols]), 0.0)
        acc = acc + _dot((hid * hid).astype(BF16), w2_ref[cols, :])
    x2 = x1 + mod_ref[0, 5:6, :] * acc
    ms = jnp.mean(x2 * x2, axis=-1, keepdims=True)
    o_ref[0] = x2 * lax.rsqrt(ms + NORM_EPS) * gf_ref[...]


def _out_mlp(ret, hf, hb, z, x, mod3, g2, gf, wo, w1, w2):
    b, t, d = x.shape
    gcol, gatecol = 3, 5
    tile = lambda w, c: pl.BlockSpec((1, ROW_TILE, w), lambda i, j: (i, j, c))
    return pl.pallas_call(
        _out_kernel,
        name="out_mlp",
        grid=(b, t // ROW_TILE),
        in_specs=[tile(RET_WIDTH, 0), tile(RET_WIDTH, gcol), tile(LRU_WIDTH, 0), tile(LRU_WIDTH, 0),
                  tile(LRU_WIDTH, gatecol), tile(d, 0),
                  pl.BlockSpec((1, N_MOD, d), lambda i, j: (i, 0, 0)),
                  _resident((1, d)), _resident((1, d)),
                  _resident(wo.shape), _resident(w1.shape), _resident(w2.shape)],
        out_specs=tile(d, 0),
        out_shape=jax.ShapeDtypeStruct((b, t, d), F32),
        compiler_params=_params("arbitrary", "arbitrary"),
    )(ret, z, hf, hb, z, x, mod3, g2, gf, wo, w1, w2)


def _rotary_tables(t_len):
    rows = t_len // GRID_W
    row = jnp.repeat(jnp.arange(rows, dtype=F32), GRID_W)
    col = jnp.tile(jnp.arange(GRID_W, dtype=F32), rows)
    n_freq = HEAD_DIM // 4
    inv = ROPE_BASE ** (-jnp.arange(n_freq, dtype=F32) / n_freq)
    ang = jnp.concatenate([row[:, None] * inv, col[:, None] * inv], axis=-1)
    cos, sin = jnp.cos(ang), jnp.sin(ang)
    return jnp.concatenate([cos, cos], axis=-1), jnp.concatenate([-sin, sin], axis=-1)


def _pair_gate_weights(wa, wx, ba, bx):
    def pair_blocks(w):
        w = w.reshape(N_PAIRS, 2, LRU_BLOCK_DIM, LRU_BLOCK_DIM)
        zero = jnp.zeros_like(w[:, 0])
        top = jnp.concatenate([w[:, 0], zero], axis=-1)
        bot = jnp.concatenate([zero, w[:, 1]], axis=-1)
        return jnp.concatenate([top, bot], axis=-2)

    wg = jnp.concatenate([pair_blocks(wa[0]), pair_blocks(wx[0]),
                          pair_blocks(wa[1]), pair_blocks(wx[1])], axis=-1)
    bias = lambda v: v.reshape(N_PAIRS, 1, LRU_PAIR)
    bg = jnp.concatenate([bias(ba[0]), bias(bx[0]), bias(ba[1]), bias(bx[1])], axis=-1)
    return wg.astype(BF16), bg


def kernel(x, c, ctx, c_ctx, w_ada, b_ada, norm1_g, norm2_g, w_in, ret_decay, conv_w, conv_b,
           lru_wa, lru_ba, lru_wx, lru_bx, lru_lambda, w_out, w_mlp1, w_mlp2, final_g):
    b, t_len, d = x.shape
    depth = w_ada.shape[0]
    assert depth == 1 and b + 1 <= MOD_ROWS
    assert t_len % ROW_TILE == 0 and t_len % LRU_CHUNK == 0 and t_len % RET_CHUNK == 0
    l = 0
    cos2, sin2 = _rotary_tables(t_len)

    c_rows = jnp.zeros((MOD_ROWS, d), F32).at[:b].set(c).at[b].set(c_ctx)
    mod = _modulation(c_rows, w_ada[l], b_ada[l][None, :])
    mod3 = mod.reshape(MOD_ROWS, N_MOD, d)

    w_in_bf = w_in[l].astype(BF16)
    g1 = norm1_g[l][None, :]
    dec = jnp.broadcast_to(ret_decay[l][:, :, None, None], (2, RET_HEADS, SUBLANES, LANES))
    cw, cb = conv_w[l], conv_b[l][None, :]
    wg, bg = _pair_gate_weights(lru_wa[l], lru_wx[l], lru_ba[l], lru_bx[l])
    lam = lru_lambda[l]

    s_f, s_b, lru_s = _context(ctx, mod3, g1, w_in_bf, dec, cw, cb, wg, bg, lam)
    z = _in_proj(x, mod3, g1, w_in_bf, cos2, sin2)
    ret = _retention(z, s_f, s_b, dec)
    hf, hb = _rglru(z, cw, cb, wg, bg, lam, jnp.transpose(lru_s, (1, 0, 2)))
    return _out_mlp(ret, hf, hb, z, x, mod3, norm2_g[l][None, :], final_g[None, :],
                    w_out[l].astype(BF16), w_mlp1[l].astype(BF16), w_mlp2[l].astype(BF16))
```

```python
import functools
import math

import jax
import jax.numpy as jnp
from jax import lax
from jax.experimental import pallas as pl
from jax.experimental.pallas import tpu as pltpu

F32 = jnp.float32
BF16 = jnp.bfloat16

RET_HEADS = 4
HEAD_DIM = 128
RET_WIDTH = RET_HEADS * HEAD_DIM
LRU_WIDTH = 512
LRU_BLOCK_DIM = 64
LRU_C = 8.0
CONV_WIDTH = 4
N_MOD = 6
GRID_W = 64
ROPE_BASE = 10000.0
NORM_EPS = 1e-6
LOG2_E = 1.0 / math.log(2.0)

LANES = 128
SUBLANES = 8
VMEM_LIMIT_BYTES = 56 * 1024 * 1024

RET_CHUNK = 256
RET_UNROLL = 4
ROW_TILE = 512
LRU_CHUNK = 128
LRU_SUB = 16
W_IN_BLOCKS = ("q", "k", "v", "g", "x", "gate")
Z_BLOCKS = ("q", "k", "v", "g", "gate")
MLP_CHUNK = 1024
MOD_COLS = 1536
MOD_ROWS = 16
LRU_PAIR = 2 * LRU_BLOCK_DIM
N_PAIRS = LRU_WIDTH // LRU_PAIR


def _params(*sem):
    return pltpu.CompilerParams(dimension_semantics=sem, vmem_limit_bytes=VMEM_LIMIT_BYTES)


def _resident(shape):
    nd = len(shape)
    return pl.BlockSpec(shape, lambda *_: (0,) * nd, pipeline_mode=pl.Buffered(1))


def _sigmoid(x):
    return 0.5 * jnp.tanh(0.5 * x) + 0.5


def _softplus(x):
    return jnp.maximum(x, 0.0) + jnp.log(1.0 + jnp.exp(-jnp.abs(x)))


def _rms_modulate(x, g, shift, scale):
    ms = jnp.mean(x * x, axis=-1, keepdims=True)
    return (x * lax.rsqrt(ms + NORM_EPS) * g) * (1.0 + scale) + shift


def _dot(a, b):
    return jnp.dot(a, b, preferred_element_type=F32)


def _dot_tn(a, b):
    return lax.dot_general(a, b, (((0,), (0,)), ((), ())), preferred_element_type=F32)


def _dot_nt(a, b):
    return lax.dot_general(a, b, (((1,), (1,)), ((), ())), preferred_element_type=F32)


def _mod_kernel(c_ref, w_ref, b_ref, o_ref):
    c = c_ref[...]
    s = c * _sigmoid(c)
    o_ref[...] = jnp.dot(s, w_ref[...], preferred_element_type=F32,
                         precision=lax.Precision.HIGHEST) + b_ref[...]


def _modulation(c_rows, w_ada, b_ada):
    d, n = w_ada.shape
    return pl.pallas_call(
        _mod_kernel,
        name="modulation",
        grid=(n // MOD_COLS,),
        in_specs=[pl.BlockSpec((MOD_ROWS, d), lambda j: (0, 0)),
                  pl.BlockSpec((d, MOD_COLS), lambda j: (0, j)),
                  pl.BlockSpec((1, MOD_COLS), lambda j: (0, j))],
        out_specs=pl.BlockSpec((MOD_ROWS, MOD_COLS), lambda j: (0, j)),
        out_shape=jax.ShapeDtypeStruct((MOD_ROWS, n), F32),
        compiler_params=_params("arbitrary"),
    )(c_rows, w_ada, b_ada)


def _gate_pair(xc_p, wg_ref, bg_ref, lam_ref, p, d):
    col = 2 * d * LRU_PAIR
    t = jnp.tanh(_dot(xc_p.astype(BF16), wg_ref[p, :, col:col + 2 * LRU_PAIR])
                 + bg_ref[p, :, col:col + 2 * LRU_PAIR])
    t_r, t_i = t[:, :LRU_PAIR], t[:, LRU_PAIR:]
    half = (-0.5 * LRU_C * LOG2_E) * _softplus(-lam_ref[d:d + 1, p * LRU_PAIR:(p + 1) * LRU_PAIR])
    log2_a = t_r * half + half
    a = jnp.exp2(log2_a)
    s = 1.0 - a * a
    u = jnp.where(s > 0.0, s * lax.rsqrt(s), 0.0) * (0.5 * xc_p)
    return log2_a, a, u * t_i + u


def _conv_taps(x, x_m1, x_p1, x_p2, cw_ref, cb_ref):
    return (cb_ref[...] + x_m1 * cw_ref[0:1, :] + x * cw_ref[1:2, :]
            + x_p1 * cw_ref[2:3, :] + x_p2 * cw_ref[3:4, :])


def _ctx_kernel(ctx_ref, mod_ref, g1_ref, wk_ref, wv_ref, wx_ref, dec_ref, cw_ref, cb_ref,
                wg_ref, bg_ref, lam_ref, sf_ref, sb_ref, lru_ref):
    x = ctx_ref[0]
    n_ctx = x.shape[0]
    h = _rms_modulate(x, g1_ref[...], mod_ref[0, 0:1, :], mod_ref[0, 1:2, :]).astype(BF16)
    k = _dot(h, wk_ref[...]) * (HEAD_DIM ** -0.5)
    v = _dot(h, wv_ref[...])
    xr = _dot(h, wx_ref[...])

    row = lax.broadcasted_iota(jnp.int32, (n_ctx, HEAD_DIM), 0).astype(F32)
    for hd in range(RET_HEADS):
        sl = slice(hd * HEAD_DIM, (hd + 1) * HEAD_DIM)
        lg_f = -_softplus(-dec_ref[0, hd])[0:1, :]
        lg_b = -_softplus(-dec_ref[1, hd])[0:1, :]
        kh, vh = k[:, sl], v[:, sl].astype(BF16)
        w_f = jnp.exp(lg_f * (n_ctx - 1.0 - row))
        w_b = jnp.exp(lg_b * row)
        sf_ref[0, hd] = _dot_tn((kh * w_f).astype(BF16), vh)
        sb_ref[0, hd] = _dot_tn((kh * w_b).astype(BF16), vh)

    rowc = lax.broadcasted_iota(jnp.int32, (n_ctx, LRU_WIDTH), 0)
    x_m1 = jnp.where(rowc >= 1, pltpu.roll(xr, 1, axis=0), 0.0)
    x_p1 = jnp.where(rowc < n_ctx - 1, pltpu.roll(xr, n_ctx - 1, axis=0), 0.0)
    x_p2 = jnp.where(rowc < n_ctx - 2, pltpu.roll(xr, n_ctx - 2, axis=0), 0.0)
    xc = _conv_taps(xr, x_m1, x_p1, x_p2, cw_ref, cb_ref)

    r_i = lax.broadcasted_iota(jnp.int32, (n_ctx, n_ctx), 0)
    c_i = lax.broadcasted_iota(jnp.int32, (n_ctx, n_ctx), 1)
    later = (c_i > r_i).astype(F32)
    earlier = (c_i < r_i).astype(F32)
    for p in range(N_PAIRS):
        sl = slice(p * LRU_PAIR, (p + 1) * LRU_PAIR)
        for d, tri in ((0, later), (1, earlier)):
            log2_a, _, inp = _gate_pair(xc[:, sl], wg_ref, bg_ref, lam_ref, p, d)
            cum = jnp.dot(tri, log2_a, preferred_element_type=F32, precision=lax.Precision.HIGHEST)
            lru_ref[0, d:d + 1, sl] = jnp.sum(jnp.exp2(cum) * inp, axis=0, keepdims=True)


def _context(ctx, mod3, g1, w_in_bf, dec, cw, cb, wg, bg, lam):
    b, n_ctx, d = ctx.shape
    kcol, vcol, xcol = (W_IN_BLOCKS.index(n) for n in ("k", "v", "x"))
    state = jax.ShapeDtypeStruct((b, RET_HEADS, HEAD_DIM, HEAD_DIM), F32)
    return pl.pallas_call(
        _ctx_kernel,
        name="context",
        grid=(b,),
        in_specs=[pl.BlockSpec((1, n_ctx, d), lambda i: (i, 0, 0)),
                  pl.BlockSpec((1, N_MOD, d), lambda i: (b, 0, 0)),
                  pl.BlockSpec((1, d), lambda i: (0, 0)),
                  pl.BlockSpec((d, RET_WIDTH), lambda i: (0, kcol)),
                  pl.BlockSpec((d, RET_WIDTH), lambda i: (0, vcol)),
                  pl.BlockSpec((d, LRU_WIDTH), lambda i: (0, xcol)),
                  pl.BlockSpec(dec.shape, lambda i: (0, 0, 0, 0)),
                  pl.BlockSpec(cw.shape, lambda i: (0, 0)),
                  pl.BlockSpec(cb.shape, lambda i: (0, 0)),
                  pl.BlockSpec(wg.shape, lambda i: (0, 0, 0)),
                  pl.BlockSpec(bg.shape, lambda i: (0, 0, 0)),
                  pl.BlockSpec(lam.shape, lambda i: (0, 0))],
        out_specs=[pl.BlockSpec((1, RET_HEADS, HEAD_DIM, HEAD_DIM), lambda i: (i, 0, 0, 0)),
                   pl.BlockSpec((1, RET_HEADS, HEAD_DIM, HEAD_DIM), lambda i: (i, 0, 0, 0)),
                   pl.BlockSpec((1, 2, LRU_WIDTH), lambda i: (i, 0, 0))],
        out_shape=[state, state, jax.ShapeDtypeStruct((b, 2, LRU_WIDTH), F32)],
        compiler_params=_params("arbitrary"),
    )(ctx, mod3, g1, w_in_bf, w_in_bf, w_in_bf, dec, cw, cb, wg, bg, lam)


def _inproj_kernel(x_ref, mod_ref, g1_ref, w_ref, cos_ref, sin_ref, z_ref, xl_ref):
    h = _rms_modulate(x_ref[0], g1_ref[...], mod_ref[0, 0:1, :], mod_ref[0, 1:2, :]).astype(BF16)
    cos = cos_ref[...]
    sin = sin_ref[...]
    for n in range(len(W_IN_BLOCKS)):
        zc = _dot(h, w_ref[:, n * RET_WIDTH:(n + 1) * RET_WIDTH])
        if n == W_IN_BLOCKS.index("x"):
            xl_ref[...] = zc
            continue
        out0 = Z_BLOCKS.index(W_IN_BLOCKS[n]) * RET_WIDTH
        if W_IN_BLOCKS[n] in ("q", "k"):
            scale = 1.0 if W_IN_BLOCKS[n] == "q" else HEAD_DIM ** -0.5
            for hd in range(RET_HEADS):
                blk = zc[:, hd * HEAD_DIM:(hd + 1) * HEAD_DIM]
                rot = blk * cos + pltpu.roll(blk, HEAD_DIM // 2, axis=1) * sin
                z_ref[0, :, out0 + hd * HEAD_DIM:out0 + (hd + 1) * HEAD_DIM] = (
                    (rot * scale).astype(BF16))
        else:
            z_ref[0, :, out0:out0 + RET_WIDTH] = zc.astype(BF16)


def _in_proj(x, mod3, g1, w_in_bf, cos2, sin2):
    b, t, d = x.shape
    n_cols = w_in_bf.shape[1]
    z_cols = len(Z_BLOCKS) * RET_WIDTH
    return pl.pallas_call(
        _inproj_kernel,
        name="in_proj",
        grid=(b, t // ROW_TILE),
        in_specs=[pl.BlockSpec((1, ROW_TILE, d), lambda i, j: (i, j, 0)),
                  pl.BlockSpec((1, N_MOD, d), lambda i, j: (i, 0, 0)),
                  _resident((1, d)),
                  _resident((d, n_cols)),
                  pl.BlockSpec((ROW_TILE, HEAD_DIM), lambda i, j: (j, 0)),
                  pl.BlockSpec((ROW_TILE, HEAD_DIM), lambda i, j: (j, 0))],
        out_specs=[pl.BlockSpec((1, ROW_TILE, z_cols), lambda i, j: (i, j, 0)),
                   pl.BlockSpec((ROW_TILE, LRU_WIDTH), lambda i, j: (j, i))],
        out_shape=[jax.ShapeDtypeStruct((b, t, z_cols), BF16),
                   jax.ShapeDtypeStruct((t, b * LRU_WIDTH), F32)],
        compiler_params=_params("arbitrary", "arbitrary"),
    )(x, mod3, g1, w_in_bf, cos2, sin2)


def _ret_kernel(q_ref, k_ref, v_ref, sf_ref, sb_ref, dec_ref, o_ref, u_ref, s_ref, dec_s, w_s, gc_s):
    c = RET_CHUNK
    n_chunks = q_ref.shape[1] // c

    @pl.when(pl.program_id(1) == 0)
    def _():
        lg_f = -_softplus(-dec_ref[0, 0])[0:1, :]
        lg_b = -_softplus(-dec_ref[1, 0])[0:1, :]
        lg_fc = jnp.concatenate([lg_f] * (c // LANES), axis=1)
        lg_bc = jnp.concatenate([lg_b] * (c // LANES), axis=1)
        row = lax.broadcasted_iota(jnp.int32, (c, HEAD_DIM), 0).astype(F32)
        rel = (lax.broadcasted_iota(jnp.int32, (c, c), 0)
               - lax.broadcasted_iota(jnp.int32, (c, c), 1)).astype(F32)
        dec_s[...] = (jnp.where(rel >= 0, jnp.exp(lg_fc * jnp.maximum(rel, 0.0)), 0.0)
                      + jnp.where(rel <= 0, jnp.exp(lg_bc * jnp.maximum(-rel, 0.0)), 0.0))
        w_s[0] = jnp.exp(lg_f * (c - 1.0 - row)).astype(BF16)
        w_s[1] = jnp.exp(lg_b * row).astype(BF16)
        w_s[2] = jnp.exp(lg_f * (row + 1.0)).astype(BF16)
        w_s[3] = jnp.exp(lg_b * (c - row)).astype(BF16)
        gc_s[0] = jnp.broadcast_to(jnp.exp(lg_f * float(c)), (SUBLANES, LANES))
        gc_s[1] = jnp.broadcast_to(jnp.exp(lg_b * float(c)), (SUBLANES, LANES))

    ws_f, ws_b, wq_f, wq_b = w_s[0], w_s[1], w_s[2], w_s[3]
    gc_f = gc_s[0, 0:1, :]
    gc_b = gc_s[1, 0:1, :]

    def chunk(n):
        return pl.ds(pl.multiple_of(n * c, c), c)

    def incr(n, carry):
        kn = k_ref[0, chunk(n), :]
        vn = v_ref[0, chunk(n), :]
        u_ref[n] = _dot_tn(kn, jnp.concatenate([vn * ws_f, vn * ws_b], axis=1))
        return carry

    lax.fori_loop(0, n_chunks, incr, 0, unroll=RET_UNROLL)

    def scan(n, carry):
        s_f, s_b = carry
        m = n_chunks - 1 - n
        s_ref[n, 0:HEAD_DIM, :] = s_f.astype(BF16)
        s_ref[m, HEAD_DIM:2 * HEAD_DIM, :] = s_b.astype(BF16)
        s_f = gc_f * s_f + u_ref[n, :, 0:HEAD_DIM]
        s_b = gc_b * s_b + u_ref[m, :, HEAD_DIM:2 * HEAD_DIM]
        return s_f, s_b

    lax.fori_loop(0, n_chunks, scan, (sf_ref[0, 0], sb_ref[0, 0]))

    def out(n, carry):
        qn = q_ref[0, chunk(n), :]
        kn = k_ref[0, chunk(n), :]
        vn = v_ref[0, chunk(n), :]
        scores = (_dot_nt(qn, kn) * dec_s[...]).astype(BF16)
        qw = jnp.concatenate([qn * wq_f, qn * wq_b], axis=1)
        o_ref[0, chunk(n), :] = (_dot(scores, vn) + _dot(qw, s_ref[n])).astype(BF16)
        return carry

    lax.fori_loop(0, n_chunks, out, 0, unroll=RET_UNROLL)


def _retention(z, s_f, s_b, dec):
    b, t, _ = z.shape
    n_chunks = t // RET_CHUNK
    qcol, kcol, vcol = (Z_BLOCKS.index(n) * RET_HEADS for n in ("q", "k", "v"))
    seq = lambda c0: pl.BlockSpec((1, t, HEAD_DIM), lambda h, i: (i, 0, c0 + h))
    state = pl.BlockSpec((1, 1, HEAD_DIM, HEAD_DIM), lambda h, i: (i, h, 0, 0))
    return pl.pallas_call(
        _ret_kernel,
        name="retention",
        grid=(RET_HEADS, b),
        in_specs=[seq(qcol), seq(kcol), seq(vcol), state, state,
                  pl.BlockSpec((2, 1, SUBLANES, LANES), lambda h, i: (0, h, 0, 0))],
        out_specs=pl.BlockSpec((1, t, HEAD_DIM), lambda h, i: (i, 0, h)),
        out_shape=jax.ShapeDtypeStruct((b, t, RET_WIDTH), BF16),
        scratch_shapes=[pltpu.VMEM((n_chunks, HEAD_DIM, 2 * HEAD_DIM), F32),
                        pltpu.VMEM((n_chunks, 2 * HEAD_DIM, HEAD_DIM), BF16),
                        pltpu.VMEM((RET_CHUNK, RET_CHUNK), F32),
                        pltpu.VMEM((4, RET_CHUNK, HEAD_DIM), BF16),
                        pltpu.VMEM((2, SUBLANES, LANES), F32)],
        compiler_params=_params("arbitrary", "arbitrary"),
    )(z, z, z, s_f, s_b, dec)


def _lru_kernel(xf_ref, pf_ref, nf_ref, xb_ref, pb_ref, nb_ref, cw_ref, cb_ref, wg_ref, bg_ref,
                lam_ref, h0_ref, hf_ref, hb_ref, a_s, b_s, hc_s):
    i = pl.program_id(0)
    n_steps = pl.num_programs(0)
    tc, n_batch, width = xf_ref.shape
    sub = LRU_SUB
    n_sub = tc // sub

    @pl.when(i == 0)
    def _():
        hc_s[...] = h0_ref[...]

    cw = [cw_ref[j:j + 1, :] for j in range(CONV_WIDTH)]
    cb = cb_ref[...]

    def fill(d, x_ref, prev_ref, next_ref, has_prev, has_next):
        prev = jnp.where(has_prev, prev_ref[...], 0.0)
        nxt = jnp.where(has_next, next_ref[...], 0.0)

        def sub_block(j, carry):
            t0 = pl.multiple_of(j * sub, sub)
            x = x_ref[pl.ds(t0, sub)]
            before = jnp.where(j == 0, prev, x_ref[pl.ds(jnp.maximum(t0 - 1, 0), 1)])
            after = jnp.where(j == n_sub - 1, nxt, x_ref[pl.ds(jnp.minimum(t0 + sub, tc - 2), 2)])
            x_m1 = jnp.concatenate([before, x[:sub - 1]], axis=0)
            x_p1 = jnp.concatenate([x[1:], after[0:1]], axis=0)
            x_p2 = jnp.concatenate([x[2:], after], axis=0)
            xc = cb + x_m1 * cw[0] + x * cw[1] + x_p1 * cw[2] + x_p2 * cw[3]
            xc = xc.reshape(sub * n_batch, width)
            for p in range(N_PAIRS):
                sl = slice(p * LRU_PAIR, (p + 1) * LRU_PAIR)
                _, a, inp = _gate_pair(xc[:, sl], wg_ref, bg_ref, lam_ref, p, d)
                a_s[d, pl.ds(t0, sub), :, sl] = a.reshape(sub, n_batch, LRU_PAIR)
                b_s[d, pl.ds(t0, sub), :, sl] = inp.reshape(sub, n_batch, LRU_PAIR)
            return carry

        lax.fori_loop(0, n_sub, sub_block, 0)

    not_first = i > 0
    not_last = i < n_steps - 1
    fill(0, xf_ref, pf_ref, nf_ref, not_first, not_last)
    fill(1, xb_ref, pb_ref, nb_ref, not_last, not_first)

    def step(t, carry):
        h_f, h_b = carry
        tb = tc - 1 - t
        h_f = a_s[0, t] * h_f + b_s[0, t]
        hf_ref[t] = h_f
        h_b = a_s[1, tb] * h_b + b_s[1, tb]
        hb_ref[tb] = h_b
        return h_f, h_b

    h_f, h_b = lax.fori_loop(0, tc, step, (hc_s[0], hc_s[1]), unroll=8)
    hc_s[0] = h_f
    hc_s[1] = h_b


def _rglru(x_lru, cw, cb, wg, bg, lam, h0):
    t, b, w = x_lru.shape
    n = t // LRU_CHUNK
    fwd = lambda i: i
    bwd = lambda i: n - 1 - i
    main = lambda f: pl.BlockSpec((LRU_CHUNK, b, w), lambda i: (f(i), 0, 0))
    prev = lambda f: pl.BlockSpec((1, b, w), lambda i: (jnp.maximum(f(i) * LRU_CHUNK - 1, 0), 0, 0))
    nxt = lambda f: pl.BlockSpec(
        (2, b, w), lambda i: (jnp.minimum((f(i) + 1) * (LRU_CHUNK // 2), t // 2 - 1), 0, 0))
    const = lambda a: pl.BlockSpec(a.shape, lambda i: (0,) * a.ndim)
    return pl.pallas_call(
        _lru_kernel,
        name="rglru",
        grid=(n,),
        in_specs=[main(fwd), prev(fwd), nxt(fwd), main(bwd), prev(bwd), nxt(bwd),
                  const(cw), const(cb), const(wg), const(bg), const(lam), const(h0)],
        out_specs=[main(fwd), main(bwd)],
        out_shape=[jax.ShapeDtypeStruct((t, b, w), F32)] * 2,
        scratch_shapes=[pltpu.VMEM((2, LRU_CHUNK, b, w), F32),
                        pltpu.VMEM((2, LRU_CHUNK, b, w), F32),
                        pltpu.VMEM((2, b, w), F32)],
        compiler_params=_params("arbitrary"),
    )(x_lru, x_lru, x_lru, x_lru, x_lru, x_lru, cw, cb, wg, bg, lam, h0)


def _out_kernel(ret_ref, g_ref, hf_ref, hb_ref, gate_ref, x_ref, mod_ref, g2_ref, gf_ref,
                wo_ref, w1_ref, w2_ref, o_ref):
    heads = []
    for hd in range(RET_HEADS):
        hs = slice(hd * HEAD_DIM, (hd + 1) * HEAD_DIM)
        o = ret_ref[0, :, hs].astype(F32)
        oc = o - jnp.mean(o, axis=-1, keepdims=True)
        var = jnp.mean(oc * oc, axis=-1, keepdims=True)
        g = g_ref[0, :, hs].astype(F32)
        heads.append((g * _sigmoid(g) * (oc * lax.rsqrt(var + NORM_EPS))).astype(BF16))
    ret = jnp.concatenate(heads, axis=1)
    gate = gate_ref[0].astype(F32)
    gelu = 0.5 * gate * (1.0 + jnp.tanh(math.sqrt(2.0 / math.pi)
                                        * (gate + 0.044715 * (gate * gate * gate))))
    lru = ((hf_ref[...] + hb_ref[...]) * gelu).astype(BF16)
    y = _dot(ret, wo_ref[0:RET_WIDTH, :]) + _dot(lru, wo_ref[RET_WIDTH:, :])
    x1 = x_ref[0] + mod_ref[0, 2:3, :] * y
    h2 = _rms_modulate(x1, g2_ref[...], mod_ref[0, 3:4, :], mod_ref[0, 4:5, :]).astype(BF16)
    acc = jnp.zeros_like(x1)
    for j in range(w1_ref.shape[1] // MLP_CHUNK):
        cols = slice(j * MLP_CHUNK, (j + 1) * MLP_CHUNK)
        hid = jnp.maximum(_dot(h2, w1_ref[:, cols]), 0.0)
        acc = acc + _dot((hid * hid).astype(BF16), w2_ref[cols, :])
    x2 = x1 + mod_ref[0, 5:6, :] * acc
    ms = jnp.mean(x2 * x2, axis=-1, keepdims=True)
    o_ref[0] = x2 * lax.rsqrt(ms + NORM_EPS) * gf_ref[...]


def _out_mlp(ret, hf, hb, z, x, mod3, g2, gf, wo, w1, w2):
    b, t, d = x.shape
    gcol, gatecol = Z_BLOCKS.index("g"), Z_BLOCKS.index("gate")
    scan = pl.BlockSpec((ROW_TILE, LRU_WIDTH), lambda i, j: (j, i))
    tile = lambda w, c: pl.BlockSpec((1, ROW_TILE, w), lambda i, j: (i, j, c))
    return pl.pallas_call(
        _out_kernel,
        name="out_mlp",
        grid=(b, t // ROW_TILE),
        in_specs=[tile(RET_WIDTH, 0), tile(RET_WIDTH, gcol), scan, scan,
                  tile(LRU_WIDTH, gatecol), tile(d, 0),
                  pl.BlockSpec((1, N_MOD, d), lambda i, j: (i, 0, 0)),
                  _resident((1, d)), _resident((1, d)),
                  _resident(wo.shape), _resident(w1.shape), _resident(w2.shape)],
        out_specs=tile(d, 0),
        out_shape=jax.ShapeDtypeStruct((b, t, d), F32),
        compiler_params=_params("arbitrary", "arbitrary"),
    )(ret, z, hf, hb, z, x, mod3, g2, gf, wo, w1, w2)


def _rotary_tables(t_len):
    rows = t_len // GRID_W
    n_freq = HEAD_DIM // 4
    inv = ROPE_BASE ** (-jnp.arange(n_freq, dtype=F32) / n_freq)
    ang = jnp.arange(max(rows, GRID_W), dtype=F32)[:, None] * inv

    def per_token(f):
        by_row = jnp.repeat(f[:rows], GRID_W, axis=0)
        by_col = jnp.tile(f[:GRID_W], (rows, 1))
        return jnp.concatenate([by_row, by_col], axis=-1)

    cos, sin = per_token(jnp.cos(ang)), per_token(jnp.sin(ang))
    return jnp.concatenate([cos, cos], axis=-1), jnp.concatenate([-sin, sin], axis=-1)


def _pair_gate_weights(wa, wx, ba, bx):
    def pair_blocks(w):
        w = w.reshape(N_PAIRS, 2, LRU_BLOCK_DIM, LRU_BLOCK_DIM)
        zero = jnp.zeros_like(w[:, 0])
        top = jnp.concatenate([w[:, 0], zero], axis=-1)
        bot = jnp.concatenate([zero, w[:, 1]], axis=-1)
        return jnp.concatenate([top, bot], axis=-2)

    wg = jnp.concatenate([pair_blocks(wa[0]), pair_blocks(wx[0]),
                          pair_blocks(wa[1]), pair_blocks(wx[1])], axis=-1)
    bias = lambda v: v.reshape(N_PAIRS, 1, LRU_PAIR)
    bg = jnp.concatenate([bias(ba[0]), bias(bx[0]), bias(ba[1]), bias(bx[1])], axis=-1)
    return (0.5 * wg).astype(BF16), 0.5 * bg


def kernel(x, c, ctx, c_ctx, w_ada, b_ada, norm1_g, norm2_g, w_in, ret_decay, conv_w, conv_b,
           lru_wa, lru_ba, lru_wx, lru_bx, lru_lambda, w_out, w_mlp1, w_mlp2, final_g):
    b, t_len, d = x.shape
    depth = w_ada.shape[0]
    assert depth == 1 and b + 1 <= MOD_ROWS
    assert t_len % ROW_TILE == 0 and t_len % LRU_CHUNK == 0 and t_len % RET_CHUNK == 0
    l = 0
    cos2, sin2 = _rotary_tables(t_len)

    c_rows = jnp.zeros((MOD_ROWS, d), F32).at[:b].set(c).at[b].set(c_ctx)
    mod = _modulation(c_rows, w_ada[l], b_ada[l][None, :])
    mod3 = mod.reshape(MOD_ROWS, N_MOD, d)

    w_in_bf = w_in[l].astype(BF16)
    g1 = norm1_g[l][None, :]
    dec = jnp.broadcast_to(ret_decay[l][:, :, None, None], (2, RET_HEADS, SUBLANES, LANES))
    cw, cb = conv_w[l], conv_b[l][None, :]
    wg, bg = _pair_gate_weights(lru_wa[l], lru_wx[l], lru_ba[l], lru_bx[l])
    lam = lru_lambda[l]

    s_f, s_b, lru_s = _context(ctx, mod3, g1, w_in_bf, dec, cw, cb, wg, bg, lam)
    z, x_lru = _in_proj(x, mod3, g1, w_in_bf, cos2, sin2)
    ret = _retention(z, s_f, s_b, dec)
    hf, hb = _rglru(x_lru.reshape(t_len, b, LRU_WIDTH), cw, cb, wg, bg, lam,
                    jnp.transpose(lru_s, (1, 0, 2)))
    hf, hb = (h.reshape(t_len, b * LRU_WIDTH) for h in (hf, hb))
    return _out_mlp(ret, hf, hb, z, x, mod3, norm2_g[l][None, :], final_g[None, :],
                    w_out[l].astype(BF16), w_mlp1[l].astype(BF16), w_mlp2[l].astype(BF16))
```

```python
import math

import jax
import jax.numpy as jnp
from jax import lax
from jax.experimental import pallas as pl
from jax.experimental.pallas import tpu as pltpu

F32 = jnp.float32
BF16 = jnp.bfloat16

RET_HEADS = 4
HEAD_DIM = 128
RET_WIDTH = RET_HEADS * HEAD_DIM
LRU_WIDTH = 512
LRU_BLOCK_DIM = 64
LRU_C = 8.0
CONV_WIDTH = 4
N_MOD = 6
GRID_W = 64
ROPE_BASE = 10000.0
NORM_EPS = 1e-6
LOG2_E = 1.0 / math.log(2.0)

LANES = 128
SUBLANES = 8
VMEM_LIMIT_BYTES = 56 * 1024 * 1024

RET_CHUNK = 256
RET_UNROLL = 4
ROW_TILE = 512
LRU_CHUNK = 128
LRU_SUB = 16
W_IN_BLOCKS = ("q", "k", "v", "g", "x", "gate")
CAST_STEPS = 64
MLP_CHUNK = 1024
MOD_COLS = 1536
MOD_ROWS = 16
LRU_PAIR = 2 * LRU_BLOCK_DIM
N_PAIRS = LRU_WIDTH // LRU_PAIR


def _params(*sem):
    return pltpu.CompilerParams(dimension_semantics=sem, vmem_limit_bytes=VMEM_LIMIT_BYTES)


def _resident(shape):
    nd = len(shape)
    return pl.BlockSpec(shape, lambda *_: (0,) * nd, pipeline_mode=pl.Buffered(1))


def _sigmoid(x):
    return 0.5 * jnp.tanh(0.5 * x) + 0.5


def _softplus(x):
    return jnp.maximum(x, 0.0) + jnp.log(1.0 + jnp.exp(-jnp.abs(x)))


def _rms_modulate(x, g, shift, scale):
    ms = jnp.mean(x * x, axis=-1, keepdims=True)
    return (x * lax.rsqrt(ms + NORM_EPS) * g) * (1.0 + scale) + shift


def _dot(a, b):
    return jnp.dot(a, b, preferred_element_type=F32)


def _dot_tn(a, b):
    return lax.dot_general(a, b, (((0,), (0,)), ((), ())), preferred_element_type=F32)


def _dot_nt(a, b):
    return lax.dot_general(a, b, (((1,), (1,)), ((), ())), preferred_element_type=F32)


def _mod_kernel(c_ref, w_ref, b_ref, o_ref):
    c = c_ref[...]
    s = c * _sigmoid(c)
    o_ref[...] = jnp.dot(s, w_ref[...], preferred_element_type=F32,
                         precision=lax.Precision.HIGHEST) + b_ref[...]


def _modulation(c_rows, w_ada, b_ada):
    d, n = w_ada.shape
    return pl.pallas_call(
        _mod_kernel,
        name="modulation",
        grid=(n // MOD_COLS,),
        in_specs=[pl.BlockSpec((MOD_ROWS, d), lambda j: (0, 0)),
                  pl.BlockSpec((d, MOD_COLS), lambda j: (0, j)),
                  pl.BlockSpec((1, MOD_COLS), lambda j: (0, j))],
        out_specs=pl.BlockSpec((MOD_ROWS, MOD_COLS), lambda j: (0, j)),
        out_shape=jax.ShapeDtypeStruct((MOD_ROWS, n), F32),
        compiler_params=_params("arbitrary"),
    )(c_rows, w_ada, b_ada)


def _gate_pair(xc_p, wg_ref, bg_ref, lam_ref, p, d):
    col = 2 * d * LRU_PAIR
    t = jnp.tanh(_dot(xc_p.astype(BF16), wg_ref[p, :, col:col + 2 * LRU_PAIR])
                 + bg_ref[p, :, col:col + 2 * LRU_PAIR])
    t_r, t_i = t[:, :LRU_PAIR], t[:, LRU_PAIR:]
    half = (-0.5 * LRU_C * LOG2_E) * _softplus(-lam_ref[d:d + 1, p * LRU_PAIR:(p + 1) * LRU_PAIR])
    log2_a = t_r * half + half
    a = jnp.exp2(log2_a)
    s = 1.0 - a * a
    u = jnp.where(s > 0.0, s * lax.rsqrt(s), 0.0) * (0.5 * xc_p)
    return log2_a, a, u * t_i + u


def _conv_taps(x, x_m1, x_p1, x_p2, cw_ref, cb_ref):
    return (cb_ref[...] + x_m1 * cw_ref[0:1, :] + x * cw_ref[1:2, :]
            + x_p1 * cw_ref[2:3, :] + x_p2 * cw_ref[3:4, :])


def _ctx_kernel(ctx_ref, mod_ref, g1_ref, wk_ref, wv_ref, wx_ref, dec_ref, cw_ref, cb_ref,
                wg_ref, bg_ref, lam_ref, sf_ref, sb_ref, lru_ref):
    x = ctx_ref[0]
    n_ctx = x.shape[0]
    h = _rms_modulate(x, g1_ref[...], mod_ref[0, 0:1, :], mod_ref[0, 1:2, :]).astype(BF16)
    k = _dot(h, wk_ref[...]) * (HEAD_DIM ** -0.5)
    v = _dot(h, wv_ref[...])
    xr = _dot(h, wx_ref[...])

    row = lax.broadcasted_iota(jnp.int32, (n_ctx, HEAD_DIM), 0).astype(F32)
    for hd in range(RET_HEADS):
        sl = slice(hd * HEAD_DIM, (hd + 1) * HEAD_DIM)
        lg_f = -_softplus(-dec_ref[0, hd])[0:1, :]
        lg_b = -_softplus(-dec_ref[1, hd])[0:1, :]
        kh, vh = k[:, sl], v[:, sl].astype(BF16)
        w_f = jnp.exp(lg_f * (n_ctx - 1.0 - row))
        w_b = jnp.exp(lg_b * row)
        sf_ref[0, hd] = _dot_tn((kh * w_f).astype(BF16), vh)
        sb_ref[0, hd] = _dot_tn((kh * w_b).astype(BF16), vh)

    rowc = lax.broadcasted_iota(jnp.int32, (n_ctx, LRU_WIDTH), 0)
    x_m1 = jnp.where(rowc >= 1, pltpu.roll(xr, 1, axis=0), 0.0)
    x_p1 = jnp.where(rowc < n_ctx - 1, pltpu.roll(xr, n_ctx - 1, axis=0), 0.0)
    x_p2 = jnp.where(rowc < n_ctx - 2, pltpu.roll(xr, n_ctx - 2, axis=0), 0.0)
    xc = _conv_taps(xr, x_m1, x_p1, x_p2, cw_ref, cb_ref)

    r_i = lax.broadcasted_iota(jnp.int32, (n_ctx, n_ctx), 0)
    c_i = lax.broadcasted_iota(jnp.int32, (n_ctx, n_ctx), 1)
    later = (c_i > r_i).astype(F32)
    earlier = (c_i < r_i).astype(F32)
    for p in range(N_PAIRS):
        sl = slice(p * LRU_PAIR, (p + 1) * LRU_PAIR)
        for d, tri in ((0, later), (1, earlier)):
            log2_a, _, inp = _gate_pair(xc[:, sl], wg_ref, bg_ref, lam_ref, p, d)
            cum = jnp.dot(tri, log2_a, preferred_element_type=F32, precision=lax.Precision.HIGHEST)
            lru_ref[0, d:d + 1, sl] = jnp.sum(jnp.exp2(cum) * inp, axis=0, keepdims=True)


def _context(ctx, mod3, g1, w_in_bf, dec, cw, cb, wg, bg, lam):
    b, n_ctx, d = ctx.shape
    kcol, vcol, xcol = (W_IN_BLOCKS.index(n) for n in ("k", "v", "x"))
    state = jax.ShapeDtypeStruct((b, RET_HEADS, HEAD_DIM, HEAD_DIM), F32)
    return pl.pallas_call(
        _ctx_kernel,
        name="context",
        grid=(b,),
        in_specs=[pl.BlockSpec((1, n_ctx, d), lambda i: (i, 0, 0)),
                  pl.BlockSpec((1, N_MOD, d), lambda i: (b, 0, 0)),
                  pl.BlockSpec((1, d), lambda i: (0, 0)),
                  pl.BlockSpec((d, RET_WIDTH), lambda i: (0, kcol)),
                  pl.BlockSpec((d, RET_WIDTH), lambda i: (0, vcol)),
                  pl.BlockSpec((d, LRU_WIDTH), lambda i: (0, xcol)),
                  pl.BlockSpec(dec.shape, lambda i: (0, 0, 0, 0)),
                  pl.BlockSpec(cw.shape, lambda i: (0, 0)),
                  pl.BlockSpec(cb.shape, lambda i: (0, 0)),
                  pl.BlockSpec(wg.shape, lambda i: (0, 0, 0)),
                  pl.BlockSpec(bg.shape, lambda i: (0, 0, 0)),
                  pl.BlockSpec(lam.shape, lambda i: (0, 0))],
        out_specs=[pl.BlockSpec((1, RET_HEADS, HEAD_DIM, HEAD_DIM), lambda i: (i, 0, 0, 0)),
                   pl.BlockSpec((1, RET_HEADS, HEAD_DIM, HEAD_DIM), lambda i: (i, 0, 0, 0)),
                   pl.BlockSpec((1, 2, LRU_WIDTH), lambda i: (i, 0, 0))],
        out_shape=[state, state, jax.ShapeDtypeStruct((b, 2, LRU_WIDTH), F32)],
        compiler_params=_params("arbitrary"),
    )(ctx, mod3, g1, w_in_bf, w_in_bf, w_in_bf, dec, cw, cb, wg, bg, lam)


def _inproj_kernel(x_ref, xp_ref, xn_ref, mod_ref, g1_ref, w_ref, cos_ref, sin_ref, cw_ref, cb_ref,
                   wo_ref, w1_ref, w2_ref,
                   qkv_ref, gg_ref, xc_ref, wo_bf_ref, w1_bf_ref, w2_bf_ref, x_s):
    j = pl.program_id(1)
    rows = x_ref.shape[1]
    g1, shift, scale = g1_ref[...], mod_ref[0, 0:1, :], mod_ref[0, 1:2, :]
    h = _rms_modulate(x_ref[0], g1, shift, scale).astype(BF16)

    def block(name):
        n = W_IN_BLOCKS.index(name)
        return w_ref[:, n * RET_WIDTH:(n + 1) * RET_WIDTH]

    halo = jnp.concatenate([xp_ref[0], xn_ref[0]], axis=0)
    hh = _rms_modulate(halo, g1, shift, scale).astype(BF16)
    xl = _dot(jnp.concatenate([h, hh], axis=0), block("x"))
    prev = jnp.where(j > 0, xl[rows + SUBLANES - 1:rows + SUBLANES, :], 0.0)
    nxt = jnp.where(j < pl.num_programs(1) - 1, xl[rows + SUBLANES:rows + SUBLANES + 2, :], 0.0)
    lo = SUBLANES
    for p in range(N_PAIRS):
        sl = slice(p * LRU_PAIR, (p + 1) * LRU_PAIR)
        x_s[p, lo:lo + rows, :] = xl[:rows, sl]
        x_s[p, lo - 1:lo, :] = prev[:, sl]
        x_s[p, lo + rows:lo + rows + 2, :] = nxt[:, sl]
    def conv(p):
        sl = slice(p * LRU_PAIR, (p + 1) * LRU_PAIR)
        xc = cb_ref[:, sl]
        for tap in range(CONV_WIDTH):
            xc = xc + x_s[p, lo - 1 + tap:lo - 1 + tap + rows, :] * cw_ref[tap:tap + 1, sl]
        xc_ref[:, sl] = xc

    cos = cos_ref[...]
    sin = sin_ref[...]
    for n, name in enumerate(("q", "k", "v")):
        zc = _dot(h, block(name))
        for hd in range(RET_HEADS):
            blk = zc[:, hd * HEAD_DIM:(hd + 1) * HEAD_DIM]
            if name == "q":
                blk = blk * cos + pltpu.roll(blk, HEAD_DIM // 2, axis=1) * sin
            elif name == "k":
                blk = (blk * cos + pltpu.roll(blk, HEAD_DIM // 2, axis=1) * sin) * HEAD_DIM ** -0.5
            qkv_ref[0, n * RET_HEADS + hd] = blk.astype(BF16)
        conv(n)
    gg_ref[0, :, 0:RET_WIDTH] = _dot(h, block("g")).astype(BF16)
    conv(N_PAIRS - 1)
    gg_ref[0, :, RET_WIDTH:] = _dot(h, block("gate")).astype(BF16)

    wo_bf_ref[...] = wo_ref[...].astype(BF16)
    w1_bf_ref[...] = w1_ref[...].astype(BF16)
    w2_bf_ref[...] = w2_ref[...].astype(BF16)


def _in_proj(x, mod3, g1, w_in_bf, cos2, sin2, cw, cb, wo, w1, w2):
    b, t, d = x.shape
    n_t = t // ROW_TILE
    assert b * n_t == CAST_STEPS
    halo_blocks = ROW_TILE // SUBLANES
    step = lambda i, j: (i * n_t + j, 0)
    cast_in = lambda w: pl.BlockSpec((w.shape[0] // CAST_STEPS, w.shape[1]), step)
    cast_out = lambda w: jax.ShapeDtypeStruct(w.shape, BF16)
    return pl.pallas_call(
        _inproj_kernel,
        name="in_proj",
        grid=(b, n_t),
        in_specs=[pl.BlockSpec((1, ROW_TILE, d), lambda i, j: (i, j, 0)),
                  pl.BlockSpec((1, SUBLANES, d),
                               lambda i, j: (i, jnp.maximum(j * halo_blocks - 1, 0), 0)),
                  pl.BlockSpec((1, SUBLANES, d),
                               lambda i, j: (i, jnp.minimum((j + 1) * halo_blocks,
                                                            n_t * halo_blocks - 1), 0)),
                  pl.BlockSpec((1, N_MOD, d), lambda i, j: (i, 0, 0)),
                  _resident((1, d)),
                  _resident(w_in_bf.shape),
                  pl.BlockSpec((ROW_TILE, HEAD_DIM), lambda i, j: (j, 0)),
                  pl.BlockSpec((ROW_TILE, HEAD_DIM), lambda i, j: (j, 0)),
                  _resident(cw.shape), _resident(cb.shape),
                  cast_in(wo), cast_in(w1), cast_in(w2)],
        out_specs=[pl.BlockSpec((1, 3 * RET_HEADS, ROW_TILE, HEAD_DIM), lambda i, j: (i, 0, j, 0)),
                   pl.BlockSpec((1, ROW_TILE, RET_WIDTH + LRU_WIDTH), lambda i, j: (i, j, 0)),
                   pl.BlockSpec((ROW_TILE, LRU_WIDTH), lambda i, j: (j, i)),
                   cast_in(wo), cast_in(w1), cast_in(w2)],
        out_shape=[jax.ShapeDtypeStruct((b, 3 * RET_HEADS, t, HEAD_DIM), BF16),
                   jax.ShapeDtypeStruct((b, t, RET_WIDTH + LRU_WIDTH), BF16),
                   jax.ShapeDtypeStruct((t, b * LRU_WIDTH), F32),
                   cast_out(wo), cast_out(w1), cast_out(w2)],
        scratch_shapes=[pltpu.VMEM((N_PAIRS, ROW_TILE + 2 * SUBLANES, LANES), F32)],
        compiler_params=_params("arbitrary", "arbitrary"),
    )(x, x, x, mod3, g1, w_in_bf, cos2, sin2, cw, cb, wo, w1, w2)


def _ret_kernel(q_ref, k_ref, v_ref, sf_ref, sb_ref, dec_ref, o_ref, u_ref, s_ref, dec_s, w_s, gc_s):
    c = RET_CHUNK
    n_chunks = q_ref.shape[2] // c

    @pl.when(pl.program_id(1) == 0)
    def _():
        lg_f = -_softplus(-dec_ref[0, 0])[0:1, :]
        lg_b = -_softplus(-dec_ref[1, 0])[0:1, :]
        lg_fc = jnp.concatenate([lg_f] * (c // LANES), axis=1)
        lg_bc = jnp.concatenate([lg_b] * (c // LANES), axis=1)
        row = lax.broadcasted_iota(jnp.int32, (c, HEAD_DIM), 0).astype(F32)
        rel = (lax.broadcasted_iota(jnp.int32, (c, c), 0)
               - lax.broadcasted_iota(jnp.int32, (c, c), 1)).astype(F32)
        dec_s[...] = (jnp.where(rel >= 0, jnp.exp(lg_fc * jnp.maximum(rel, 0.0)), 0.0)
                      + jnp.where(rel <= 0, jnp.exp(lg_bc * jnp.maximum(-rel, 0.0)), 0.0))
        w_s[0] = jnp.exp(lg_f * (c - 1.0 - row)).astype(BF16)
        w_s[1] = jnp.exp(lg_b * row).astype(BF16)
        w_s[2] = jnp.exp(lg_f * (row + 1.0)).astype(BF16)
        w_s[3] = jnp.exp(lg_b * (c - row)).astype(BF16)
        gc_s[0] = jnp.broadcast_to(jnp.exp(lg_f * float(c)), (SUBLANES, LANES))
        gc_s[1] = jnp.broadcast_to(jnp.exp(lg_b * float(c)), (SUBLANES, LANES))

    ws_f, ws_b, wq_f, wq_b = w_s[0], w_s[1], w_s[2], w_s[3]
    gc_f = gc_s[0, 0:1, :]
    gc_b = gc_s[1, 0:1, :]

    def chunk(n):
        return pl.ds(pl.multiple_of(n * c, c), c)

    def incr(n, carry):
        kn = k_ref[0, 0, chunk(n), :]
        vn = v_ref[0, 0, chunk(n), :]
        u_ref[n] = _dot_tn(kn, jnp.concatenate([vn * ws_f, vn * ws_b], axis=1))
        return carry

    lax.fori_loop(0, n_chunks, incr, 0, unroll=RET_UNROLL)

    def scan(n, carry):
        s_f, s_b = carry
        m = n_chunks - 1 - n
        s_ref[n, 0:HEAD_DIM, :] = s_f.astype(BF16)
        s_ref[m, HEAD_DIM:2 * HEAD_DIM, :] = s_b.astype(BF16)
        s_f = gc_f * s_f + u_ref[n, :, 0:HEAD_DIM]
        s_b = gc_b * s_b + u_ref[m, :, HEAD_DIM:2 * HEAD_DIM]
        return s_f, s_b

    lax.fori_loop(0, n_chunks, scan, (sf_ref[0, 0], sb_ref[0, 0]))

    def out(n, carry):
        qn = q_ref[0, 0, chunk(n), :]
        kn = k_ref[0, 0, chunk(n), :]
        vn = v_ref[0, 0, chunk(n), :]
        scores = (_dot_nt(qn, kn) * dec_s[...]).astype(BF16)
        qw = jnp.concatenate([qn * wq_f, qn * wq_b], axis=1)
        o_ref[0, 0, chunk(n), :] = (_dot(scores, vn) + _dot(qw, s_ref[n])).astype(BF16)
        return carry

    lax.fori_loop(0, n_chunks, out, 0, unroll=RET_UNROLL)


def _retention(qkv, s_f, s_b, dec):
    b, _, t, _ = qkv.shape
    n_chunks = t // RET_CHUNK
    seq = lambda n: pl.BlockSpec((1, 1, t, HEAD_DIM), lambda h, i: (i, n * RET_HEADS + h, 0, 0))
    state = pl.BlockSpec((1, 1, HEAD_DIM, HEAD_DIM), lambda h, i: (i, h, 0, 0))
    return pl.pallas_call(
        _ret_kernel,
        name="retention",
        grid=(RET_HEADS, b),
        in_specs=[seq(0), seq(1), seq(2), state, state,
                  pl.BlockSpec((2, 1, SUBLANES, LANES), lambda h, i: (0, h, 0, 0))],
        out_specs=pl.BlockSpec((1, 1, t, HEAD_DIM), lambda h, i: (i, h, 0, 0)),
        out_shape=jax.ShapeDtypeStruct((b, RET_HEADS, t, HEAD_DIM), BF16),
        scratch_shapes=[pltpu.VMEM((n_chunks, HEAD_DIM, 2 * HEAD_DIM), F32),
                        pltpu.VMEM((n_chunks, 2 * HEAD_DIM, HEAD_DIM), BF16),
                        pltpu.VMEM((RET_CHUNK, RET_CHUNK), F32),
                        pltpu.VMEM((4, RET_CHUNK, HEAD_DIM), BF16),
                        pltpu.VMEM((2, SUBLANES, LANES), F32)],
        compiler_params=_params("arbitrary", "arbitrary"),
    )(qkv, qkv, qkv, s_f, s_b, dec)


def _lru_kernel(xf_ref, xb_ref, wg_ref, bg_ref, lam_ref, h0_ref, hf_ref, hb_ref, a_s, b_s, hc_s):
    tc, n_batch, width = xf_ref.shape
    sub = LRU_SUB

    @pl.when(pl.program_id(0) == 0)
    def _():
        hc_s[...] = h0_ref[...]

    def fill(d, x_ref):
        def sub_block(j, carry):
            steps = pl.ds(pl.multiple_of(j * sub, sub), sub)
            xc = x_ref[steps].reshape(sub * n_batch, width)
            for p in range(N_PAIRS):
                sl = slice(p * LRU_PAIR, (p + 1) * LRU_PAIR)
                _, a, inp = _gate_pair(xc[:, sl], wg_ref, bg_ref, lam_ref, p, d)
                a_s[d, steps, :, sl] = a.reshape(sub, n_batch, LRU_PAIR)
                b_s[d, steps, :, sl] = inp.reshape(sub, n_batch, LRU_PAIR)
            return carry

        lax.fori_loop(0, tc // sub, sub_block, 0, unroll=8)

    fill(0, xf_ref)
    fill(1, xb_ref)

    def step(t, carry):
        h_f, h_b = carry
        tb = tc - 1 - t
        h_f = a_s[0, t] * h_f + b_s[0, t]
        hf_ref[t] = h_f
        h_b = a_s[1, tb] * h_b + b_s[1, tb]
        hb_ref[tb] = h_b
        return h_f, h_b

    h_f, h_b = lax.fori_loop(0, tc, step, (hc_s[0], hc_s[1]), unroll=8)
    hc_s[0] = h_f
    hc_s[1] = h_b


def _rglru(xc, wg, bg, lam, h0):
    t, b, w = xc.shape
    n = t // LRU_CHUNK
    fwd = pl.BlockSpec((LRU_CHUNK, b, w), lambda i: (i, 0, 0))
    bwd = pl.BlockSpec((LRU_CHUNK, b, w), lambda i: (n - 1 - i, 0, 0))
    const = lambda a: pl.BlockSpec(a.shape, lambda i: (0,) * a.ndim)
    return pl.pallas_call(
        _lru_kernel,
        name="rglru",
        grid=(n,),
        in_specs=[fwd, bwd, const(wg), const(bg), const(lam), const(h0)],
        out_specs=[fwd, bwd],
        out_shape=[jax.ShapeDtypeStruct((t, b, w), F32)] * 2,
        scratch_shapes=[pltpu.VMEM((2, LRU_CHUNK, b, w), F32),
                        pltpu.VMEM((2, LRU_CHUNK, b, w), F32),
                        pltpu.VMEM((2, b, w), F32)],
        compiler_params=_params("arbitrary"),
    )(xc, xc, wg, bg, lam, h0)


def _out_kernel(ret_ref, gg_ref, hf_ref, hb_ref, x_ref, mod_ref, g2_ref, gf_ref,
                wo_ref, w1_ref, w2_ref, o_ref):
    heads = []
    for hd in range(RET_HEADS):
        hs = slice(hd * HEAD_DIM, (hd + 1) * HEAD_DIM)
        o = ret_ref[0, hd].astype(F32)
        oc = o - jnp.mean(o, axis=-1, keepdims=True)
        var = jnp.mean(oc * oc, axis=-1, keepdims=True)
        g = gg_ref[0, :, hs].astype(F32)
        heads.append((g * _sigmoid(g) * (oc * lax.rsqrt(var + NORM_EPS))).astype(BF16))
    ret = jnp.concatenate(heads, axis=1)
    gate = gg_ref[0, :, RET_WIDTH:].astype(F32)
    gelu = 0.5 * gate * (1.0 + jnp.tanh(math.sqrt(2.0 / math.pi)
                                        * (gate + 0.044715 * (gate * gate * gate))))
    lru = ((hf_ref[...] + hb_ref[...]) * gelu).astype(BF16)
    y = _dot(ret, wo_ref[0:RET_WIDTH, :]) + _dot(lru, wo_ref[RET_WIDTH:, :])
    x1 = x_ref[0] + mod_ref[0, 2:3, :] * y
    h2 = _rms_modulate(x1, g2_ref[...], mod_ref[0, 3:4, :], mod_ref[0, 4:5, :]).astype(BF16)
    acc = jnp.zeros_like(x1)
    for j in range(w1_ref.shape[1] // MLP_CHUNK):
        cols = slice(j * MLP_CHUNK, (j + 1) * MLP_CHUNK)
        hid = jnp.maximum(_dot(h2, w1_ref[:, cols]), 0.0)
        acc = acc + _dot((hid * hid).astype(BF16), w2_ref[cols, :])
    x2 = x1 + mod_ref[0, 5:6, :] * acc
    ms = jnp.mean(x2 * x2, axis=-1, keepdims=True)
    o_ref[0] = x2 * lax.rsqrt(ms + NORM_EPS) * gf_ref[...]


def _out_mlp(ret, gg, hf, hb, x, mod3, g2, gf, wo, w1, w2):
    b, t, d = x.shape
    scan = pl.BlockSpec((ROW_TILE, LRU_WIDTH), lambda i, j: (j, i))
    tile = lambda w: pl.BlockSpec((1, ROW_TILE, w), lambda i, j: (i, j, 0))
    return pl.pallas_call(
        _out_kernel,
        name="out_mlp",
        grid=(b, t // ROW_TILE),
        in_specs=[pl.BlockSpec((1, RET_HEADS, ROW_TILE, HEAD_DIM), lambda i, j: (i, 0, j, 0)),
                  tile(gg.shape[2]), scan, scan, tile(d),
                  pl.BlockSpec((1, N_MOD, d), lambda i, j: (i, 0, 0)),
                  _resident((1, d)), _resident((1, d)),
                  _resident(wo.shape), _resident(w1.shape), _resident(w2.shape)],
        out_specs=tile(d),
        out_shape=jax.ShapeDtypeStruct((b, t, d), F32),
        compiler_params=_params("arbitrary", "arbitrary"),
    )(ret, gg, hf, hb, x, mod3, g2, gf, wo, w1, w2)


def _rotary_tables(t_len):
    rows = t_len // GRID_W
    n_freq = HEAD_DIM // 4
    inv = ROPE_BASE ** (-jnp.arange(n_freq, dtype=F32) / n_freq)
    ang = jnp.arange(max(rows, GRID_W), dtype=F32)[:, None] * inv

    def per_token(f):
        by_row = jnp.repeat(f[:rows], GRID_W, axis=0)
        by_col = jnp.tile(f[:GRID_W], (rows, 1))
        return jnp.concatenate([by_row, by_col], axis=-1)

    cos, sin = per_token(jnp.cos(ang)), per_token(jnp.sin(ang))
    return jnp.concatenate([cos, cos], axis=-1), jnp.concatenate([-sin, sin], axis=-1)


def _pair_gate_weights(wa, wx, ba, bx):
    def pair_blocks(w):
        w = w.reshape(N_PAIRS, 2, LRU_BLOCK_DIM, LRU_BLOCK_DIM)
        zero = jnp.zeros_like(w[:, 0])
        top = jnp.concatenate([w[:, 0], zero], axis=-1)
        bot = jnp.concatenate([zero, w[:, 1]], axis=-1)
        return jnp.concatenate([top, bot], axis=-2)

    wg = jnp.concatenate([pair_blocks(wa[0]), pair_blocks(wx[0]),
                          pair_blocks(wa[1]), pair_blocks(wx[1])], axis=-1)
    bias = lambda v: v.reshape(N_PAIRS, 1, LRU_PAIR)
    bg = jnp.concatenate([bias(ba[0]), bias(bx[0]), bias(ba[1]), bias(bx[1])], axis=-1)
    return (0.5 * wg).astype(BF16), 0.5 * bg


def kernel(x, c, ctx, c_ctx, w_ada, b_ada, norm1_g, norm2_g, w_in, ret_decay, conv_w, conv_b,
           lru_wa, lru_ba, lru_wx, lru_bx, lru_lambda, w_out, w_mlp1, w_mlp2, final_g):
    b, t_len, d = x.shape
    depth = w_ada.shape[0]
    assert depth == 1 and b + 1 <= MOD_ROWS
    assert t_len % ROW_TILE == 0 and t_len % LRU_CHUNK == 0 and t_len % RET_CHUNK == 0
    l = 0
    cos2, sin2 = _rotary_tables(t_len)

    c_rows = jnp.zeros((MOD_ROWS, d), F32).at[:b].set(c).at[b].set(c_ctx)
    mod = _modulation(c_rows, w_ada[l], b_ada[l][None, :])
    mod3 = mod.reshape(MOD_ROWS, N_MOD, d)

    w_in_bf = w_in[l].astype(BF16)
    g1 = norm1_g[l][None, :]
    dec = jnp.broadcast_to(ret_decay[l][:, :, None, None], (2, RET_HEADS, SUBLANES, LANES))
    cw, cb = conv_w[l], conv_b[l][None, :]
    wg, bg = _pair_gate_weights(lru_wa[l], lru_wx[l], lru_ba[l], lru_bx[l])
    lam = lru_lambda[l]

    s_f, s_b, lru_s = _context(ctx, mod3, g1, w_in_bf, dec, cw, cb, wg, bg, lam)
    qkv, gg, xc, wo, w1, w2 = _in_proj(x, mod3, g1, w_in_bf, cos2, sin2, cw, cb,
                                       w_out[l], w_mlp1[l], w_mlp2[l])
    ret = _retention(qkv, s_f, s_b, dec)
    hf, hb = _rglru(xc.reshape(t_len, b, LRU_WIDTH), wg, bg, lam, jnp.transpose(lru_s, (1, 0, 2)))
    hf, hb = (h.reshape(t_len, b * LRU_WIDTH) for h in (hf, hb))
    return _out_mlp(ret, gg, hf, hb, x, mod3, norm2_g[l][None, :], final_g[None, :], wo, w1, w2)
```
